```python
import math
import jax, jax.numpy as jnp
from jax import lax
import numpy as np

D_MODEL = 1024
BATCH = 8
SEQ = 4096
DEPTH = 2
DEC_BATCH = 32
DEC_SEQ = 8
PAST_LEN = 16384
PAGE_SIZE = 128

D_MIX = D_MODEL
A_HEADS = 4
A_DHEAD = 64
A_WIDTH = A_HEADS * 2 * A_DHEAD
B_GROUPS = 4
B_DGROUP = 64
B_WIDTH = B_GROUPS * B_DGROUP
CHUNK = 128
C_HEADS = 4
C_DHEAD = 64
C_WIDTH = C_HEADS * C_DHEAD
IN_COLS = 4 * A_WIDTH + 3 * B_WIDTH + 4 * C_WIDTH
Q_BLOCK = 128
EPS = 1e-6
LAMBDA_STD = 0.1

kernel_name = "hybrid_diffattn_sgu_stickbreak_decode_step"


def rmsnorm(x, g):
    xf = x.astype(jnp.float32)
    y = xf * lax.rsqrt(jnp.mean(xf * xf, axis=-1, keepdims=True) + EPS)
    return (y * g.astype(jnp.float32)).astype(x.dtype)


def layernorm(x, g, b):
    xf = x.astype(jnp.float32)
    mu = jnp.mean(xf, axis=-1, keepdims=True)
    var = jnp.mean(jnp.square(xf - mu), axis=-1, keepdims=True)
    y = (xf - mu) * lax.rsqrt(var + EPS)
    return (y * g.astype(jnp.float32) + b.astype(jnp.float32)).astype(x.dtype)


def sweep_query_blocks(fn, q, q_pos):
    B, T = q.shape[0], q.shape[1]
    qb = min(Q_BLOCK, T)
    n = -(-T // qb)
    pad = n * qb - T
    q = jnp.pad(q, [(0, 0), (0, pad)] + [(0, 0)] * (q.ndim - 2))
    q_pos = jnp.pad(q_pos, (0, pad), mode='edge')
    qs = jnp.moveaxis(q.reshape((B, n, qb) + q.shape[2:]), 1, 0)
    ps = q_pos.reshape(n, qb)
    out = lax.map(lambda a: fn(a[0], a[1]), (qs, ps))
    out = jnp.moveaxis(out, 0, 1)
    return out.reshape((B, n * qb) + out.shape[3:])[:, :T]


def diff_attention(q, k, v, q_pos, k_pos, lam):
    scale = A_DHEAD ** -0.5
    kf = k.astype(jnp.float32)
    vf = v.astype(jnp.float32)

    def block(qb, pb):
        s = jnp.einsum('bqhcd,bkhcd->bhcqk', qb.astype(jnp.float32), kf) * scale
        mask = k_pos[None, :] <= pb[:, None]
        p = jax.nn.softmax(jnp.where(mask, s, -jnp.inf), axis=-1)
        w = p[:, :, 0] - lam * p[:, :, 1]
        return jnp.einsum('bhqk,bkhe->bqhe', w, vf)

    return sweep_query_blocks(block, q, q_pos)


def stick_breaking_attention(q, k, v, q_pos, k_pos):
    scale = C_DHEAD ** -0.5
    kf = k.astype(jnp.float32)
    vf = v.astype(jnp.float32)

    def block(qb, pb):
        z = jnp.einsum('bqhd,bkhd->bhqk', qb.astype(jnp.float32), kf) * scale
        mask = k_pos[None, :] < pb[:, None]
        log_keep = jnp.where(mask, -jax.nn.softplus(z), 0.0)
        after = lax.cumsum(log_keep, axis=3, reverse=True) - log_keep
        w = jnp.where(mask, jnp.exp(jax.nn.log_sigmoid(z) + after), 0.0)
        return jnp.einsum('bhqk,bkhd->bqhd', w, vf)

    return sweep_query_blocks(block, q, q_pos)


def chunk_spatial_gating(u, vn, w_s, b_s):
    B, T = vn.shape[0], vn.shape[1]
    n = -(-T // CHUNK)
    pad = n * CHUNK - T
    vp = jnp.pad(vn, [(0, 0), (0, pad), (0, 0), (0, 0)])
    vp = vp.reshape(B, n, CHUNK, B_GROUPS, B_DGROUP)
    tri = jnp.tril(jnp.ones((CHUNK, CHUNK), dtype=bool))
    w = jnp.where(tri[None], w_s, 0.0)
    mixed = jnp.einsum('gts,bnsgd->bntgd', w, vp) + jnp.transpose(b_s)[None, None, :, :, None]
    mixed = mixed.reshape(B, n * CHUNK, B_GROUPS, B_DGROUP)[:, :T]
    return u * mixed.astype(u.dtype)


def hybrid_layer(x, past, q_pos, k_pos, lam_init, w_in, w_out, g_pre, g_post,
                 lq1, lk1, lq2, lk2, subln_g, ln_g, ln_b, sgu_w, sgu_b):
    B, T, _ = x.shape
    h = rmsnorm(x, g_pre)
    proj = jnp.einsum('btd,de->bte', h, w_in)
    sizes = [A_WIDTH] * 4 + [B_WIDTH] * 3 + [C_WIDTH] * 4
    split_points = [int(s) for s in np.cumsum(sizes)[:-1]]
    (a_q, a_k, a_v, a_g, b_u, b_v, b_g, c_q, c_k, c_v, c_g) = jnp.split(proj, split_points, axis=-1)
    a_k = a_k.reshape(B, T, A_HEADS, 2 * A_DHEAD)
    a_v = a_v.reshape(B, T, A_HEADS, 2 * A_DHEAD)
    c_k = c_k.reshape(B, T, C_HEADS, C_DHEAD)
    c_v = c_v.reshape(B, T, C_HEADS, C_DHEAD)
    if past is None:
        ak_all, av_all, ck_all, cv_all = a_k, a_v, c_k, c_v
    else:
        pk, pv, pck, pcv = past
        ak_all = jnp.concatenate([pk.astype(x.dtype), a_k], axis=1)
        av_all = jnp.concatenate([pv.astype(x.dtype), a_v], axis=1)
        ck_all = jnp.concatenate([pck.astype(x.dtype), c_k], axis=1)
        cv_all = jnp.concatenate([pcv.astype(x.dtype), c_v], axis=1)

    lam = (jnp.exp(jnp.sum(lq1.astype(jnp.float32) * lk1.astype(jnp.float32)))
           - jnp.exp(jnp.sum(lq2.astype(jnp.float32) * lk2.astype(jnp.float32))) + lam_init)
    o_a = diff_attention(a_q.reshape(B, T, A_HEADS, 2, A_DHEAD),
                         ak_all.reshape(B, -1, A_HEADS, 2, A_DHEAD), av_all, q_pos, k_pos, lam)
    o_a = rmsnorm(o_a, subln_g) * (1.0 - lam_init)
    o_a = o_a.astype(x.dtype).reshape(B, T, A_WIDTH) * jax.nn.silu(a_g)

    v_n = layernorm(b_v, ln_g, ln_b)
    o_b = chunk_spatial_gating(b_u.reshape(B, T, B_GROUPS, B_DGROUP),
                               v_n.reshape(B, T, B_GROUPS, B_DGROUP), sgu_w, sgu_b)
    o_b = o_b.reshape(B, T, B_WIDTH) * jax.nn.silu(b_g)

    o_c = stick_breaking_attention(c_q.reshape(B, T, C_HEADS, C_DHEAD), ck_all, cv_all, q_pos, k_pos)
    o_c = o_c.astype(x.dtype).reshape(B, T, C_WIDTH) * jax.nn.silu(c_g)

    mix = jnp.einsum('bte,ed->btd', jnp.concatenate([o_a, o_b, o_c], axis=-1), w_out)
    y = x + rmsnorm(mix, g_post)
    return y, (a_k, a_v, c_k, c_v, v_n)


def setup_inputs(seed: int = 0) -> dict:
    key = jax.random.key(seed)
    ks = jax.random.split(key, 24)
    n_pages = PAST_LEN // PAGE_SIZE
    n_used = DEC_BATCH * n_pages
    n_pool = (n_used * 5) // 4
    f32 = jnp.float32
    nrm = lambda k, shp: jax.random.normal(k, shp, dtype=f32)
    x_prompt = nrm(ks[0], (BATCH, SEQ, D_MODEL))
    x_sample = nrm(ks[1], (DEC_BATCH, DEC_SEQ, D_MODEL))
    cache_diff_k = nrm(ks[2], (DEPTH, n_pool, PAGE_SIZE, A_HEADS, 2 * A_DHEAD))
    cache_diff_v = nrm(ks[3], (DEPTH, n_pool, PAGE_SIZE, A_HEADS, 2 * A_DHEAD))
    cache_sb_k = nrm(ks[4], (DEPTH, n_pool, PAGE_SIZE, C_HEADS, C_DHEAD))
    cache_sb_v = nrm(ks[5], (DEPTH, n_pool, PAGE_SIZE, C_HEADS, C_DHEAD))
    perm = jax.random.permutation(ks[6], n_pool)
    page_table = perm[:n_used].reshape(DEC_BATCH, n_pages).astype(jnp.int32)
    w_in = nrm(ks[7], (DEPTH, D_MODEL, IN_COLS)) * D_MODEL ** -0.5
    w_out = nrm(ks[8], (DEPTH, D_MIX, D_MODEL)) * D_MIX ** -0.5
    norm_pre_g = 1.0 + 0.05 * nrm(ks[9], (DEPTH, D_MODEL))
    norm_post_g = 1.0 + 0.05 * nrm(ks[10], (DEPTH, D_MODEL))
    lambda_q1 = LAMBDA_STD * nrm(ks[11], (DEPTH, A_DHEAD))
    lambda_k1 = LAMBDA_STD * nrm(ks[12], (DEPTH, A_DHEAD))
    lambda_q2 = LAMBDA_STD * nrm(ks[13], (DEPTH, A_DHEAD))
    lambda_k2 = LAMBDA_STD * nrm(ks[14], (DEPTH, A_DHEAD))
    diff_subln_g = 1.0 + 0.05 * nrm(ks[15], (DEPTH, 2 * A_DHEAD))
    sgu_ln_g = 1.0 + 0.05 * nrm(ks[16], (DEPTH, B_WIDTH))
    sgu_ln_b = 0.02 * nrm(ks[17], (DEPTH, B_WIDTH))
    sgu_w = nrm(ks[18], (DEPTH, B_GROUPS, CHUNK, CHUNK)) * CHUNK ** -0.5
    sgu_b = 1.0 + 0.02 * nrm(ks[19], (DEPTH, B_GROUPS, CHUNK))
    return {"x_prompt": x_prompt, "x_sample": x_sample,
            "cache_diff_k": cache_diff_k, "cache_diff_v": cache_diff_v,
            "cache_sb_k": cache_sb_k, "cache_sb_v": cache_sb_v,
            "page_table": page_table,
            "w_in": w_in, "w_out": w_out, "norm_pre_g": norm_pre_g, "norm_post_g": norm_post_g,
            "lambda_q1": lambda_q1, "lambda_k1": lambda_k1, "lambda_q2": lambda_q2, "lambda_k2": lambda_k2,
            "diff_subln_g": diff_subln_g, "sgu_ln_g": sgu_ln_g, "sgu_ln_b": sgu_ln_b,
            "sgu_w": sgu_w, "sgu_b": sgu_b}


def reference(x_prompt, x_sample, cache_diff_k, cache_diff_v, cache_sb_k, cache_sb_v, page_table,
              w_in, w_out, norm_pre_g, norm_post_g, lambda_q1, lambda_k1, lambda_q2, lambda_k2,
              diff_subln_g, sgu_ln_g, sgu_ln_b, sgu_w, sgu_b):
    t_prompt = x_prompt.shape[1]
    dec_b, t_sample = x_sample.shape[0], x_sample.shape[1]
    past_len = page_table.shape[1] * cache_diff_k.shape[2]
    pos_p = jnp.arange(t_prompt, dtype=jnp.int32)
    q_pos_s = past_len + jnp.arange(t_sample, dtype=jnp.int32)
    k_pos_s = jnp.arange(past_len + t_sample, dtype=jnp.int32)

    hp, hs = x_prompt, x_sample
    dkp, dvp, skp, svp = [], [], [], []
    dks, dvs, sks, svs, gvs = [], [], [], [], []
    for l in range(DEPTH):
        lam_init = 0.8 - 0.6 * math.exp(-0.3 * l)
        lw = (w_in[l], w_out[l], norm_pre_g[l], norm_post_g[l],
              lambda_q1[l], lambda_k1[l], lambda_q2[l], lambda_k2[l],
              diff_subln_g[l], sgu_ln_g[l], sgu_ln_b[l], sgu_w[l], sgu_b[l])
        hp, (ak, av, ck, cv, _) = hybrid_layer(hp, None, pos_p, pos_p, lam_init, *lw)
        dkp.append(ak); dvp.append(av); skp.append(ck); svp.append(cv)
        past = (cache_diff_k[l, page_table].reshape((dec_b, past_len) + cache_diff_k.shape[3:]),
                cache_diff_v[l, page_table].reshape((dec_b, past_len) + cache_diff_v.shape[3:]),
                cache_sb_k[l, page_table].reshape((dec_b, past_len) + cache_sb_k.shape[3:]),
                cache_sb_v[l, page_table].reshape((dec_b, past_len) + cache_sb_v.shape[3:]))
        hs, (ak, av, ck, cv, vn) = hybrid_layer(hs, past, q_pos_s, k_pos_s, lam_init, *lw)
        dks.append(ak); dvs.append(av); sks.append(ck); svs.append(cv); gvs.append(vn)

    new_diff_k_prompt = jnp.stack(dkp, 0)
    new_diff_v_prompt = jnp.stack(dvp, 0)
    new_sb_k_prompt = jnp.stack(skp, 0)
    new_sb_v_prompt = jnp.stack(svp, 0)
    new_diff_k_sample = jnp.stack(dks, 0)
    new_diff_v_sample = jnp.stack(dvs, 0)
    new_sb_k_sample = jnp.stack(sks, 0)
    new_sb_v_sample = jnp.stack(svs, 0)
    new_sgu_v_sample = jnp.stack(gvs, 0)
    return (hp, hs, new_diff_k_prompt, new_diff_v_prompt, new_sb_k_prompt, new_sb_v_prompt,
            new_diff_k_sample, new_diff_v_sample, new_sb_k_sample, new_sb_v_sample, new_sgu_v_sample)
```

```python
import functools
import math

import jax
import jax.numpy as jnp
from jax import lax
from jax.experimental import pallas as pl
from jax.experimental.pallas import tpu as pltpu

A_HEADS = 4
A_DHEAD = 64
A_WIDTH = A_HEADS * 2 * A_DHEAD
B_GROUPS = 4
B_DGROUP = 64
B_WIDTH = B_GROUPS * B_DGROUP
CHUNK = 128
C_HEADS = 4
C_DHEAD = 64
C_WIDTH = C_HEADS * C_DHEAD
IN_COLS = 4 * A_WIDTH + 3 * B_WIDTH + 4 * C_WIDTH
EPS = 1e-6
LANES = 128
NEG_BIG = -1e30
VMEM_LIMIT = 56 * 1024 * 1024

_OFF = {}
_o = 0
for _name, _w in (("a_q", A_WIDTH), ("a_k", A_WIDTH), ("a_v", A_WIDTH), ("a_g", A_WIDTH),
                  ("b_u", B_WIDTH), ("b_v", B_WIDTH), ("b_g", B_WIDTH),
                  ("c_q", C_WIDTH), ("c_k", C_WIDTH), ("c_v", C_WIDTH), ("c_g", C_WIDTH)):
    _OFF[_name] = (_o, _w)
    _o += _w

BF16 = jnp.bfloat16
F32 = jnp.float32


def _silu(x):
    return x / (1.0 + jnp.exp(-x))


def _softplus(z):
    return jnp.maximum(z, 0.0) + jnp.log(1.0 + jnp.exp(-jnp.abs(z)))


def _dot_nt(a, b):
    return lax.dot_general(a, b, (((1,), (1,)), ((), ())), preferred_element_type=F32)


def _dot(a, b):
    return jnp.dot(a, b, preferred_element_type=F32)


def _in_proj_kernel(x_ref, gpre_ref, w_ref, lng_ref, lnb_ref, wt_ref, bias_ref,
                    ak_ref, av_ref, ck_ref, cv_ref, vn_ref,
                    qa_ref, ka_ref, va_ref, ga_ref, ob_ref, qc_ref, kc_ref, vc_ref, gc_ref,
                    *, chunk_len, mm_rows):
    tm = x_ref.shape[0]
    x = x_ref[...]
    h = x * lax.rsqrt(jnp.mean(x * x, axis=-1, keepdims=True) + EPS) * gpre_ref[...]
    hb = h.astype(BF16)

    def proj(name):
        off, width = _OFF[name]
        return _dot(hb, w_ref[:, off:off + width])

    a_q = proj("a_q")
    qa_ref[...] = (a_q * (A_DHEAD ** -0.5)).astype(BF16)
    a_k = proj("a_k")
    ak_ref[...] = a_k
    ka_ref[...] = a_k.astype(BF16)
    a_v = proj("a_v")
    av_ref[...] = a_v
    va_ref[...] = a_v.astype(BF16)
    ga_ref[...] = _silu(proj("a_g")).astype(BF16)
    c_q = proj("c_q")
    qc_ref[...] = (c_q * (C_DHEAD ** -0.5)).astype(BF16)
    c_k = proj("c_k")
    ck_ref[...] = c_k
    kc_ref[...] = c_k.astype(BF16)
    c_v = proj("c_v")
    cv_ref[...] = c_v
    vc_ref[...] = c_v.astype(BF16)
    gc_ref[...] = _silu(proj("c_g")).astype(BF16)

    b_v = proj("b_v")
    mu = jnp.mean(b_v, axis=-1, keepdims=True)
    cen = b_v - mu
    var = jnp.mean(cen * cen, axis=-1, keepdims=True)
    vn = cen * lax.rsqrt(var + EPS) * lng_ref[...] + lnb_ref[...]
    vn_ref[...] = vn
    vnb = vn.astype(BF16)

    r = lax.broadcasted_iota(jnp.int32, (B_GROUPS * mm_rows, mm_rows), 0) % mm_rows
    c = lax.broadcasted_iota(jnp.int32, (B_GROUPS * mm_rows, mm_rows), 1)
    keep = (r // chunk_len == c // chunk_len) & (c % chunk_len <= r % chunk_len)
    wt = jnp.where(keep, wt_ref[...], 0.0).astype(BF16)
    grp = lax.broadcasted_iota(jnp.int32, (mm_rows, B_WIDTH), 1) // B_DGROUP
    b_u = proj("b_u")
    gate_b = _silu(proj("b_g"))
    for i in range(tm // mm_rows):
        rows = slice(i * mm_rows, (i + 1) * mm_rows)
        full = _dot(wt, vnb[rows])
        mixed = bias_ref[...]
        for g in range(B_GROUPS):
            mixed = mixed + jnp.where(grp == g, full[g * mm_rows:(g + 1) * mm_rows], 0.0)
        ob_ref[rows, :] = (b_u[rows] * mixed * gate_b[rows]).astype(BF16)


def _in_proj(x, g_pre, w_in_bf16, ln_g, ln_b, sgu_w, sgu_b, *, seq_len, block_rows):
    n, d = x.shape
    chunk_len = min(seq_len, CHUNK)
    mm_rows = CHUNK if chunk_len == CHUNK else block_rows
    assert n % block_rows == 0 and block_rows % mm_rows == 0 and mm_rows % chunk_len == 0
    assert seq_len % chunk_len == 0
    rep = mm_rows // chunk_len
    wt = jnp.tile(sgu_w[:, :chunk_len, :chunk_len], (1, rep, rep)).reshape(B_GROUPS * mm_rows, mm_rows)
    bias = jnp.repeat(jnp.tile(sgu_b[:, :chunk_len], (1, rep)).T, B_DGROUP, axis=1)

    row = lambda width: pl.BlockSpec((block_rows, width), lambda i: (i, 0))
    const = lambda shape: pl.BlockSpec(shape, lambda i: (0, 0))
    out_shapes = (
        [jax.ShapeDtypeStruct((n, wd), F32) for wd in (A_WIDTH, A_WIDTH, C_WIDTH, C_WIDTH, B_WIDTH)]
        + [jax.ShapeDtypeStruct((n, wd), BF16)
           for wd in (A_WIDTH, A_WIDTH, A_WIDTH, A_WIDTH, B_WIDTH, C_WIDTH, C_WIDTH, C_WIDTH, C_WIDTH)])
    return pl.pallas_call(
        functools.partial(_in_proj_kernel, chunk_len=chunk_len, mm_rows=mm_rows),
        grid=(n // block_rows,),
        in_specs=[row(d), const((1, d)), const((d, IN_COLS)), const((1, B_WIDTH)), const((1, B_WIDTH)),
                  const((B_GROUPS * mm_rows, mm_rows)), const((mm_rows, B_WIDTH))],
        out_specs=[row(s.shape[1]) for s in out_shapes],
        out_shape=out_shapes,
        compiler_params=pltpu.CompilerParams(dimension_semantics=("parallel",),
                                             vmem_limit_bytes=VMEM_LIMIT),
        name="in_proj",
    )(x, g_pre.reshape(1, d), w_in_bf16, ln_g.reshape(1, B_WIDTH), ln_b.reshape(1, B_WIDTH), wt, bias)


def _out_proj_kernel(oa_ref, ob_ref, oc_ref, x_ref, w_ref, g_ref, y_ref):
    mix = (_dot(oa_ref[...], w_ref[0:A_WIDTH, :])
           + _dot(ob_ref[...], w_ref[A_WIDTH:A_WIDTH + B_WIDTH, :])
           + _dot(oc_ref[...], w_ref[A_WIDTH + B_WIDTH:, :]))
    y_ref[...] = x_ref[...] + mix * lax.rsqrt(jnp.mean(mix * mix, axis=-1, keepdims=True) + EPS) * g_ref[...]


def _out_proj(o_a, o_b, o_c, x, w_out_bf16, g_post, *, block_rows):
    n, d = x.shape
    row = lambda width: pl.BlockSpec((block_rows, width), lambda i: (i, 0))
    return pl.pallas_call(
        _out_proj_kernel,
        grid=(n // block_rows,),
        in_specs=[row(A_WIDTH), row(B_WIDTH), row(C_WIDTH), row(d),
                  pl.BlockSpec(w_out_bf16.shape, lambda i: (0, 0)),
                  pl.BlockSpec((1, d), lambda i: (0, 0))],
        out_specs=row(d),
        out_shape=jax.ShapeDtypeStruct((n, d), F32),
        compiler_params=pltpu.CompilerParams(dimension_semantics=("parallel",),
                                             vmem_limit_bytes=VMEM_LIMIT),
        name="out_proj",
    )(o_a, o_b, o_c, x, w_out_bf16, g_post.reshape(1, d))


def _split_halves(q):
    lane = lax.broadcasted_iota(jnp.int32, q.shape, 1)
    zero = jnp.zeros_like(q)
    return jnp.concatenate([jnp.where(lane < LANES // 2, q, zero), jnp.where(lane >= LANES // 2, q, zero)], axis=0)


def _lambda(lq1_ref, lk1_ref, lq2_ref, lk2_ref, lam_init):
    s1 = jnp.sum(lq1_ref[...] * lk1_ref[...], axis=-1, keepdims=True)
    s2 = jnp.sum(lq2_ref[...] * lk2_ref[...], axis=-1, keepdims=True)
    return jnp.exp(s1) - jnp.exp(s2) + lam_init


def _diff_finish(acc, l, lam, subln_g, gate, lam_init):
    r = acc.shape[0] // 2
    o = acc[:r] / l[:r] - lam * (acc[r:] / l[r:])
    o = o * lax.rsqrt(jnp.mean(o * o, axis=-1, keepdims=True) + EPS) * subln_g * (1.0 - lam_init)
    return o * gate


def _triangle_tables(n_blocks, descending):
    qs, ks = [], []
    for qi in range(n_blocks):
        order = range(qi, -1, -1) if descending else range(qi + 1)
        for ki in order:
            qs.append(qi)
            ks.append(ki)
    return jnp.asarray(qs, jnp.int32), jnp.asarray(ks, jnp.int32)


def _diff_attn_kernel(qtab_ref, ktab_ref, q_ref, k_ref, v_ref, gate_ref, sub_ref,
                      lq1_ref, lk1_ref, lq2_ref, lk2_ref, o_ref,
                      qq_sc, m_sc, l_sc, acc_sc, *, lam_init):
    s_idx = pl.program_id(2)
    qi = qtab_ref[s_idx]
    ki = ktab_ref[s_idx]
    tq = q_ref.shape[0]
    tk = k_ref.shape[0]

    @pl.when(ki == 0)
    def _init():
        qq_sc[...] = _split_halves(q_ref[...])
        m_sc[...] = jnp.full(m_sc.shape, NEG_BIG, F32)
        l_sc[...] = jnp.zeros(l_sc.shape, F32)
        acc_sc[...] = jnp.zeros(acc_sc.shape, F32)

    def step(masked):
        s = _dot_nt(qq_sc[...], k_ref[...])
        if masked:
            qpos = lax.broadcasted_iota(jnp.int32, s.shape, 0) % tq
            kpos = lax.broadcasted_iota(jnp.int32, s.shape, 1)
            s = jnp.where(kpos <= qpos, s, NEG_BIG)
        m_prev = m_sc[...]
        m_new = jnp.maximum(m_prev, jnp.max(s, axis=-1, keepdims=True))
        alpha = jnp.exp(m_prev - m_new)
        p = jnp.exp(s - m_new)
        l_sc[...] = alpha * l_sc[...] + jnp.sum(p, axis=-1, keepdims=True)
        acc_sc[...] = alpha * acc_sc[...] + _dot(p.astype(BF16), v_ref[...])
        m_sc[...] = m_new

    @pl.when(ki < qi)
    def _off_diag():
        step(False)

    @pl.when(ki == qi)
    def _diag():
        step(True)
        lam = _lambda(lq1_ref, lk1_ref, lq2_ref, lk2_ref, lam_init)
        o = _diff_finish(acc_sc[...], l_sc[...], lam, sub_ref[...], gate_ref[...].astype(F32), lam_init)
        o_ref[...] = o.astype(o_ref.dtype)


def _diff_attn(q, k, v, gate, subln_g, lams, *, batch, seq_len, block, lam_init):
    n = q.shape[0]
    nb = seq_len // block
    qtab, ktab = _triangle_tables(nb, descending=False)
    qmap = lambda b, h, s, qt, kt: (b * nb + qt[s], h)
    kmap = lambda b, h, s, qt, kt: (b * nb + kt[s], h)
    cmap = lambda b, h, s, qt, kt: (0, 0)
    blk = lambda m: pl.BlockSpec((block, LANES), m)
    lam_spec = pl.BlockSpec((1, A_DHEAD), cmap)
    grid_spec = pltpu.PrefetchScalarGridSpec(
        num_scalar_prefetch=2,
        grid=(batch, A_HEADS, int(qtab.shape[0])),
        in_specs=[blk(qmap), blk(kmap), blk(kmap), blk(qmap), pl.BlockSpec((1, LANES), cmap),
                  lam_spec, lam_spec, lam_spec, lam_spec],
        out_specs=blk(qmap),
        scratch_shapes=[pltpu.VMEM((2 * block, LANES), BF16), pltpu.VMEM((2 * block, 1), F32),
                        pltpu.VMEM((2 * block, 1), F32), pltpu.VMEM((2 * block, LANES), F32)])
    return pl.pallas_call(
        functools.partial(_diff_attn_kernel, lam_init=lam_init),
        grid_spec=grid_spec,
        out_shape=jax.ShapeDtypeStruct((n, A_WIDTH), BF16),
        compiler_params=pltpu.CompilerParams(dimension_semantics=("parallel", "parallel", "arbitrary"),
                                             vmem_limit_bytes=VMEM_LIMIT),
        name="diff_attn",
    )(qtab, ktab, q, k, v, gate, subln_g.reshape(1, LANES), *[a.reshape(1, A_DHEAD) for a in lams])


def _neg_strict_upper(n):
    j = lax.broadcasted_iota(jnp.int32, (n, n), 0)
    k = lax.broadcasted_iota(jnp.int32, (n, n), 1)
    return jnp.where(j > k, -1.0, 0.0).astype(BF16)


def _sb_block(z, valid, carry, sub):
    width = z.shape[1]
    sp = _softplus(z)
    if valid is not None:
        sp = jnp.where(valid, sp, 0.0)
    tri = _neg_strict_upper(sub)
    ws = []
    for j in range(width // sub - 1, -1, -1):
        cols = slice(j * sub, (j + 1) * sub)
        sp_j = sp[:, cols]
        later = _dot(sp_j.astype(BF16), tri)
        t = z[:, cols] - sp_j + later - carry
        if valid is not None:
            t = jnp.where(valid[:, cols], t, NEG_BIG)
        ws.append(jnp.exp(t))
        carry = carry + jnp.sum(sp_j, axis=-1, keepdims=True)
    return jnp.concatenate(ws[::-1], axis=1), carry


def _sb_attn_kernel(qtab_ref, ktab_ref, q_ref, k_ref, v_ref, gate_ref, o_ref,
                    qq_sc, carry_sc, acc_sc, *, sub):
    s_idx = pl.program_id(2)
    qi = qtab_ref[s_idx]
    ki = ktab_ref[s_idx]
    tq = q_ref.shape[0]

    @pl.when(ki == qi)
    def _init():
        qq_sc[...] = _split_halves(q_ref[...])
        carry_sc[...] = jnp.zeros(carry_sc.shape, F32)
        acc_sc[...] = jnp.zeros(acc_sc.shape, F32)

    def step(masked):
        z = _dot_nt(qq_sc[...], k_ref[...])
        valid = None
        if masked:
            qpos = lax.broadcasted_iota(jnp.int32, z.shape, 0) % tq
            kpos = lax.broadcasted_iota(jnp.int32, z.shape, 1)
            valid = kpos < qpos
        w, carry = _sb_block(z, valid, carry_sc[...], sub)
        carry_sc[...] = carry
        acc_sc[...] += _dot(w.astype(BF16), v_ref[...])

    @pl.when(ki == qi)
    def _diag():
        step(True)

    @pl.when(ki < qi)
    def _off_diag():
        step(False)

    @pl.when(ki == 0)
    def _finish():
        acc = acc_sc[...]
        lane = lax.broadcasted_iota(jnp.int32, (tq, LANES), 1)
        o = jnp.where(lane < LANES // 2, acc[:tq], acc[tq:])
        o_ref[...] = (o * gate_ref[...].astype(F32)).astype(o_ref.dtype)


def _sb_attn(q, k, v, gate, *, batch, seq_len, block, sub):
    n = q.shape[0]
    nb = seq_len // block
    qtab, ktab = _triangle_tables(nb, descending=True)
    qmap = lambda b, h, s, qt, kt: (b * nb + qt[s], h)
    kmap = lambda b, h, s, qt, kt: (b * nb + kt[s], h)
    blk = lambda m: pl.BlockSpec((block, LANES), m)
    grid_spec = pltpu.PrefetchScalarGridSpec(
        num_scalar_prefetch=2,
        grid=(batch, C_HEADS // 2, int(qtab.shape[0])),
        in_specs=[blk(qmap), blk(kmap), blk(kmap), blk(qmap)],
        out_specs=blk(qmap),
        scratch_shapes=[pltpu.VMEM((2 * block, LANES), BF16), pltpu.VMEM((2 * block, 1), F32),
                        pltpu.VMEM((2 * block, LANES), F32)])
    return pl.pallas_call(
        functools.partial(_sb_attn_kernel, sub=sub),
        grid_spec=grid_spec,
        out_shape=jax.ShapeDtypeStruct((n, C_WIDTH), BF16),
        compiler_params=pltpu.CompilerParams(dimension_semantics=("parallel", "parallel", "arbitrary"),
                                             vmem_limit_bytes=VMEM_LIMIT),
        name="sb_attn",
    )(qtab, ktab, q, k, v, gate)


def _decode_kernel(pt_ref, qa_ref, kan_ref, van_ref, ga_ref, qc_ref, kcn_ref, vcn_ref, gc_ref, sub_ref,
                   lq1_ref, lk1_ref, lq2_ref, lk2_ref, *rest, pages_per_step, lam_init):
    pps = pages_per_step
    dk_refs = rest[0:pps]
    dv_refs = rest[pps:2 * pps]
    sk_refs = rest[2 * pps:3 * pps]
    sv_refs = rest[3 * pps:4 * pps]
    oa_ref, oc_ref = rest[4 * pps:4 * pps + 2]
    qqa_sc, qqc_sc, m_sc, l_sc, acca_sc, carry_sc, accc_sc = rest[4 * pps + 2:]
    j = pl.program_id(1)
    t = qa_ref.shape[0]
    page = dk_refs[0].shape[0]

    def diff_update(h, s, v_blocks):
        m_prev = m_sc[h]
        m_new = jnp.maximum(m_prev, jnp.max(s, axis=-1, keepdims=True))
        alpha = jnp.exp(m_prev - m_new)
        p = jnp.exp(s - m_new)
        l_sc[h] = alpha * l_sc[h] + jnp.sum(p, axis=-1, keepdims=True)
        pb = p.astype(BF16)
        pv = _dot(pb[:, 0:page], v_blocks[0])
        for i in range(1, len(v_blocks)):
            pv = pv + _dot(pb[:, i * page:(i + 1) * page], v_blocks[i])
        acca_sc[h] = alpha * acca_sc[h] + pv
        m_sc[h] = m_new

    def sb_update(pr, z, valid, v_block):
        w, carry = _sb_block(z, valid, carry_sc[pr], page)
        carry_sc[pr] = carry
        accc_sc[pr] += _dot(w.astype(BF16), v_block)

    @pl.when(j == 0)
    def _first():
        qpos = lax.broadcasted_iota(jnp.int32, (2 * t, page), 0) % t
        kpos = lax.broadcasted_iota(jnp.int32, (2 * t, page), 1)
        kan = kan_ref[...]
        van = van_ref[...]
        for h in range(A_HEADS):
            cols = slice(h * LANES, (h + 1) * LANES)
            qq = _split_halves(qa_ref[:, cols])
            qqa_sc[h] = qq
            m_sc[h] = jnp.full((2 * t, 1), NEG_BIG, F32)
            l_sc[h] = jnp.zeros((2 * t, 1), F32)
            acca_sc[h] = jnp.zeros((2 * t, LANES), F32)
            s = jnp.where(kpos <= qpos, _dot_nt(qq, kan[:, cols]), NEG_BIG)
            diff_update(h, s, [van[:, cols]])
        kcn = kcn_ref[...]
        vcn = vcn_ref[...]
        for pr in range(C_HEADS // 2):
            cols = slice(pr * LANES, (pr + 1) * LANES)
            qq = _split_halves(qc_ref[:, cols])
            qqc_sc[pr] = qq
            carry_sc[pr] = jnp.zeros((2 * t, 1), F32)
            accc_sc[pr] = jnp.zeros((2 * t, LANES), F32)
            sb_update(pr, _dot_nt(qq, kcn[:, cols]), kpos < qpos, vcn[:, cols])

    dk = [r[...].astype(BF16) for r in dk_refs]
    dv = [r[...].astype(BF16) for r in dv_refs]
    for h in range(A_HEADS):
        cols = slice(h * LANES, (h + 1) * LANES)
        qq = qqa_sc[h]
        s = jnp.concatenate([_dot_nt(qq, kp[:, cols]) for kp in dk], axis=1)
        diff_update(h, s, [vp[:, cols] for vp in dv])
    for i in range(pps):
        sk = sk_refs[i][...].astype(BF16)
        sv = sv_refs[i][...].astype(BF16)
        for pr in range(C_HEADS // 2):
            cols = slice(pr * LANES, (pr + 1) * LANES)
            sb_update(pr, _dot_nt(qqc_sc[pr], sk[:, cols]), None, sv[:, cols])

    @pl.when(j == pl.num_programs(1) - 1)
    def _finish():
        lam = _lambda(lq1_ref, lk1_ref, lq2_ref, lk2_ref, lam_init)
        lane = lax.broadcasted_iota(jnp.int32, (t, LANES), 1)
        for h in range(A_HEADS):
            cols = slice(h * LANES, (h + 1) * LANES)
            o = _diff_finish(acca_sc[h], l_sc[h], lam, sub_ref[...], ga_ref[:, cols].astype(F32), lam_init)
            oa_ref[:, cols] = o.astype(oa_ref.dtype)
        for pr in range(C_HEADS // 2):
            cols = slice(pr * LANES, (pr + 1) * LANES)
            acc = accc_sc[pr]
            o = jnp.where(lane < LANES // 2, acc[:t], acc[t:])
            oc_ref[:, cols] = (o * gc_ref[:, cols].astype(F32)).astype(oc_ref.dtype)


def _decode(layer, page_table, cache_dk, cache_dv, cache_sk, cache_sv,
            q_a, k_a_new, v_a_new, gate_a, q_c, k_c_new, v_c_new, gate_c, subln_g, lams,
            *, dec_batch, dec_seq, pages_per_step, lam_init):
    n_pages = page_table.shape[1]
    page = cache_dk.shape[2]
    pps = pages_per_step
    assert n_pages % pps == 0
    n_steps = n_pages // pps
    cdk = cache_dk.reshape(cache_dk.shape[:3] + (A_WIDTH,))
    cdv = cache_dv.reshape(cache_dv.shape[:3] + (A_WIDTH,))
    csk = cache_sk.reshape(cache_sk.shape[:3] + (C_WIDTH,))
    csv = cache_sv.reshape(cache_sv.shape[:3] + (C_WIDTH,))
    pt_flat = page_table.reshape(-1)

    per_seq = lambda a: a.reshape(dec_batch, dec_seq, a.shape[-1])
    pad_page = lambda a: jnp.pad(per_seq(a), ((0, 0), (0, page - dec_seq), (0, 0)))
    seq_spec = lambda rows, width: pl.BlockSpec((None, rows, width), lambda b, j, pt: (b, 0, 0))
    const_spec = lambda width: pl.BlockSpec((1, width), lambda b, j, pt: (0, 0))

    def page_spec(i, width):
        def index_map(b, j, pt):
            return (layer, pt[b * n_pages + (n_pages - 1 - (j * pps + i))], 0, 0)
        return pl.BlockSpec((None, None, page, width), index_map)

    in_specs = ([seq_spec(dec_seq, A_WIDTH), seq_spec(page, A_WIDTH), seq_spec(page, A_WIDTH),
                 seq_spec(dec_seq, A_WIDTH),
                 seq_spec(dec_seq, C_WIDTH), seq_spec(page, C_WIDTH), seq_spec(page, C_WIDTH),
                 seq_spec(dec_seq, C_WIDTH),
                 const_spec(LANES)] + [const_spec(A_DHEAD)] * 4
                + [page_spec(i, A_WIDTH) for i in range(pps)] * 2
                + [page_spec(i, C_WIDTH) for i in range(pps)] * 2)
    t2 = 2 * dec_seq
    grid_spec = pltpu.PrefetchScalarGridSpec(
        num_scalar_prefetch=1,
        grid=(dec_batch, n_steps),
        in_specs=in_specs,
        out_specs=[seq_spec(dec_seq, A_WIDTH), seq_spec(dec_seq, C_WIDTH)],
        scratch_shapes=[pltpu.VMEM((A_HEADS, t2, LANES), BF16), pltpu.VMEM((C_HEADS // 2, t2, LANES), BF16),
                        pltpu.VMEM((A_HEADS, t2, 1), F32), pltpu.VMEM((A_HEADS, t2, 1), F32),
                        pltpu.VMEM((A_HEADS, t2, LANES), F32),
                        pltpu.VMEM((C_HEADS // 2, t2, 1), F32), pltpu.VMEM((C_HEADS // 2, t2, LANES), F32)])
    o_a, o_c = pl.pallas_call(
        functools.partial(_decode_kernel, pages_per_step=pps, lam_init=lam_init),
        grid_spec=grid_spec,
        out_shape=[jax.ShapeDtypeStruct((dec_batch, dec_seq, A_WIDTH), F32),
                   jax.ShapeDtypeStruct((dec_batch, dec_seq, C_WIDTH), F32)],
        compiler_params=pltpu.CompilerParams(dimension_semantics=("parallel", "arbitrary"),
                                             vmem_limit_bytes=VMEM_LIMIT),
        name="decode_attn",
    )(pt_flat, per_seq(q_a), pad_page(k_a_new), pad_page(v_a_new), per_seq(gate_a),
      per_seq(q_c), pad_page(k_c_new), pad_page(v_c_new), per_seq(gate_c),
      subln_g.reshape(1, LANES), *[a.reshape(1, A_DHEAD) for a in lams],
      *([cdk] * pps), *([cdv] * pps), *([csk] * pps), *([csv] * pps))
    n = dec_batch * dec_seq
    return o_a.reshape(n, A_WIDTH).astype(BF16), o_c.reshape(n, C_WIDTH).astype(BF16)


PROMPT_ROWS = 512
ATTN_BLOCK = 512
SB_SUB = 256
PAGES_PER_STEP = 8


def kernel(x_prompt, x_sample, cache_diff_k, cache_diff_v, cache_sb_k, cache_sb_v, page_table, w_in, w_out, norm_pre_g, norm_post_g, lambda_q1, lambda_k1, lambda_q2, lambda_k2, diff_subln_g, sgu_ln_g, sgu_ln_b, sgu_w, sgu_b):
    batch, seq, d = x_prompt.shape
    dec_batch, dec_seq, _ = x_sample.shape
    depth = w_in.shape[0]
    w_in_b = w_in.astype(BF16)
    w_out_b = w_out.astype(BF16)
    hp = x_prompt.reshape(batch * seq, d)
    hs = x_sample.reshape(dec_batch * dec_seq, d)
    block = min(ATTN_BLOCK, seq)
    sub = min(SB_SUB, block)
    prompt_rows = min(PROMPT_ROWS, seq)
    outs_p = [[] for _ in range(4)]
    outs_s = [[] for _ in range(5)]
    for l in range(depth):
        lam_init = 0.8 - 0.6 * math.exp(-0.3 * l)
        lams = (lambda_q1[l], lambda_k1[l], lambda_q2[l], lambda_k2[l])

        (ak, av, ck, cv, _, qa, ka, va, ga, ob, qc, kc, vc, gc) = _in_proj(
            hp, norm_pre_g[l], w_in_b[l], sgu_ln_g[l], sgu_ln_b[l], sgu_w[l], sgu_b[l],
            seq_len=seq, block_rows=prompt_rows)
        oa = _diff_attn(qa, ka, va, ga, diff_subln_g[l], lams, batch=batch, seq_len=seq, block=block,
                        lam_init=lam_init)
        oc = _sb_attn(qc, kc, vc, gc, batch=batch, seq_len=seq, block=block, sub=sub)
        hp = _out_proj(oa, ob, oc, hp, w_out_b[l], norm_post_g[l], block_rows=prompt_rows)
        for dst, a, heads in zip(outs_p, (ak, av, ck, cv), (A_HEADS, A_HEADS, C_HEADS, C_HEADS)):
            dst.append(a.reshape(batch, seq, heads, a.shape[-1] // heads))

        (ak, av, ck, cv, vn, qa, ka, va, ga, ob, qc, kc, vc, gc) = _in_proj(
            hs, norm_pre_g[l], w_in_b[l], sgu_ln_g[l], sgu_ln_b[l], sgu_w[l], sgu_b[l],
            seq_len=dec_seq, block_rows=dec_batch * dec_seq)
        oa, oc = _decode(l, page_table, cache_diff_k, cache_diff_v, cache_sb_k, cache_sb_v,
                         qa, ka, va, ga, qc, kc, vc, gc, diff_subln_g[l], lams,
                         dec_batch=dec_batch, dec_seq=dec_seq, pages_per_step=PAGES_PER_STEP,
                         lam_init=lam_init)
        hs = _out_proj(oa, ob, oc, hs, w_out_b[l], norm_post_g[l], block_rows=dec_batch * dec_seq)
        for dst, a, heads in zip(outs_s, (ak, av, ck, cv), (A_HEADS, A_HEADS, C_HEADS, C_HEADS)):
            dst.append(a.reshape(dec_batch, dec_seq, heads, a.shape[-1] // heads))
        outs_s[4].append(vn.reshape(dec_batch, dec_seq, B_WIDTH))

    stack = lambda xs: jnp.stack(xs, 0)
    return (hp.reshape(batch, seq, d), hs.reshape(dec_batch, dec_seq, d),
            *[stack(o) for o in outs_p], *[stack(o) for o in outs_s])
```

```python
import functools
import math

import jax
import jax.numpy as jnp
from jax import lax
from jax.experimental import pallas as pl
from jax.experimental.pallas import tpu as pltpu

A_HEADS = 4
A_DHEAD = 64
A_WIDTH = A_HEADS * 2 * A_DHEAD
B_GROUPS = 4
B_DGROUP = 64
B_WIDTH = B_GROUPS * B_DGROUP
CHUNK = 128
C_HEADS = 4
C_DHEAD = 64
C_WIDTH = C_HEADS * C_DHEAD
IN_COLS = 4 * A_WIDTH + 3 * B_WIDTH + 4 * C_WIDTH
EPS = 1e-6
LANES = 128
NEG_BIG = -1e30
LOG2E = 1.4426950408889634
VMEM_LIMIT = 56 * 1024 * 1024

_OFF = {}
_o = 0
for _name, _w in (("a_q", A_WIDTH), ("a_k", A_WIDTH), ("a_v", A_WIDTH), ("a_g", A_WIDTH),
                  ("b_u", B_WIDTH), ("b_v", B_WIDTH), ("b_g", B_WIDTH),
                  ("c_q", C_WIDTH), ("c_k", C_WIDTH), ("c_v", C_WIDTH), ("c_g", C_WIDTH)):
    _OFF[_name] = (_o, _w)
    _o += _w

BF16 = jnp.bfloat16
F32 = jnp.float32


def _silu(x):
    return x / (1.0 + jnp.exp(-x))


def _softplus2(z):
    return jnp.maximum(z, 0.0) + jnp.log2(1.0 + jnp.exp2(-jnp.abs(z)))


def _lane_tile(x, width):
    return x if width == LANES else jnp.concatenate([x] * (width // LANES), axis=1)


def _dot_nt(a, b):
    return lax.dot_general(a, b, (((1,), (1,)), ((), ())), preferred_element_type=F32)


def _dot(a, b):
    return jnp.dot(a, b, preferred_element_type=F32)


def _in_proj_kernel(x_ref, gpre_ref, w_ref, lng_ref, lnb_ref, wt_ref, bias_ref,
                    ak_ref, av_ref, ck_ref, cv_ref, vn_ref,
                    qa_ref, ka_ref, va_ref, ga_ref, ob_ref, qc_ref, kc_ref, vc_ref, gc_ref,
                    *, chunk_len, mm_rows):
    tm = x_ref.shape[0]
    x = x_ref[...]
    h = x * lax.rsqrt(jnp.mean(x * x, axis=-1, keepdims=True) + EPS) * gpre_ref[...]
    hb = h.astype(BF16)

    def proj(name):
        off, width = _OFF[name]
        return _dot(hb, w_ref[:, off:off + width])

    a_q = proj("a_q")
    qa_ref[...] = (a_q * (A_DHEAD ** -0.5 * LOG2E)).astype(BF16)
    a_k = proj("a_k")
    ak_ref[...] = a_k
    ka_ref[...] = a_k.astype(BF16)
    a_v = proj("a_v")
    av_ref[...] = a_v
    va_ref[...] = a_v.astype(BF16)
    ga_ref[...] = _silu(proj("a_g")).astype(BF16)
    c_q = proj("c_q")
    qc_ref[...] = (c_q * (C_DHEAD ** -0.5 * LOG2E)).astype(BF16)
    c_k = proj("c_k")
    ck_ref[...] = c_k
    kc_ref[...] = c_k.astype(BF16)
    c_v = proj("c_v")
    cv_ref[...] = c_v
    vc_ref[...] = c_v.astype(BF16)
    gc_ref[...] = _silu(proj("c_g")).astype(BF16)

    b_v = proj("b_v")
    mu = jnp.mean(b_v, axis=-1, keepdims=True)
    cen = b_v - mu
    var = jnp.mean(cen * cen, axis=-1, keepdims=True)
    vn = cen * lax.rsqrt(var + EPS) * lng_ref[...] + lnb_ref[...]
    vn_ref[...] = vn
    vnb = vn.astype(BF16)

    r = lax.broadcasted_iota(jnp.int32, (B_GROUPS * mm_rows, mm_rows), 0) % mm_rows
    c = lax.broadcasted_iota(jnp.int32, (B_GROUPS * mm_rows, mm_rows), 1)
    keep = (r // chunk_len == c // chunk_len) & (c % chunk_len <= r % chunk_len)
    wt = jnp.where(keep, wt_ref[...], 0.0).astype(BF16)
    grp = lax.broadcasted_iota(jnp.int32, (mm_rows, B_WIDTH), 1) // B_DGROUP
    b_u = proj("b_u")
    gate_b = _silu(proj("b_g"))
    for i in range(tm // mm_rows):
        rows = slice(i * mm_rows, (i + 1) * mm_rows)
        full = _dot(wt, vnb[rows])
        mixed = bias_ref[...]
        for g in range(B_GROUPS):
            mixed = mixed + jnp.where(grp == g, full[g * mm_rows:(g + 1) * mm_rows], 0.0)
        ob_ref[rows, :] = (b_u[rows] * mixed * gate_b[rows]).astype(BF16)


def _in_proj(x, g_pre, w_in_bf16, ln_g, ln_b, sgu_w, sgu_b, *, seq_len, block_rows):
    n, d = x.shape
    chunk_len = min(seq_len, CHUNK)
    mm_rows = CHUNK if chunk_len == CHUNK else block_rows
    assert n % block_rows == 0 and block_rows % mm_rows == 0 and mm_rows % chunk_len == 0
    assert seq_len % chunk_len == 0
    rep = mm_rows // chunk_len
    wt = jnp.tile(sgu_w[:, :chunk_len, :chunk_len], (1, rep, rep)).reshape(B_GROUPS * mm_rows, mm_rows)
    bias = jnp.repeat(jnp.tile(sgu_b[:, :chunk_len], (1, rep)).T, B_DGROUP, axis=1)

    row = lambda width: pl.BlockSpec((block_rows, width), lambda i: (i, 0))
    const = lambda shape: pl.BlockSpec(shape, lambda i: (0, 0))
    out_shapes = (
        [jax.ShapeDtypeStruct((n, wd), F32) for wd in (A_WIDTH, A_WIDTH, C_WIDTH, C_WIDTH, B_WIDTH)]
        + [jax.ShapeDtypeStruct((n, wd), BF16)
           for wd in (A_WIDTH, A_WIDTH, A_WIDTH, A_WIDTH, B_WIDTH, C_WIDTH, C_WIDTH, C_WIDTH, C_WIDTH)])
    return pl.pallas_call(
        functools.partial(_in_proj_kernel, chunk_len=chunk_len, mm_rows=mm_rows),
        grid=(n // block_rows,),
        in_specs=[row(d), const((1, d)), const((d, IN_COLS)), const((1, B_WIDTH)), const((1, B_WIDTH)),
                  const((B_GROUPS * mm_rows, mm_rows)), const((mm_rows, B_WIDTH))],
        out_specs=[row(s.shape[1]) for s in out_shapes],
        out_shape=out_shapes,
        compiler_params=pltpu.CompilerParams(dimension_semantics=("parallel",),
                                             vmem_limit_bytes=VMEM_LIMIT),
        name="in_proj",
    )(x, g_pre.reshape(1, d), w_in_bf16, ln_g.reshape(1, B_WIDTH), ln_b.reshape(1, B_WIDTH), wt, bias)


def _out_proj_kernel(oa_ref, ob_ref, oc_ref, x_ref, w_ref, g_ref, y_ref):
    mix = (_dot(oa_ref[...], w_ref[0:A_WIDTH, :])
           + _dot(ob_ref[...], w_ref[A_WIDTH:A_WIDTH + B_WIDTH, :])
           + _dot(oc_ref[...], w_ref[A_WIDTH + B_WIDTH:, :]))
    y_ref[...] = x_ref[...] + mix * lax.rsqrt(jnp.mean(mix * mix, axis=-1, keepdims=True) + EPS) * g_ref[...]


def _out_proj(o_a, o_b, o_c, x, w_out_bf16, g_post, *, block_rows):
    n, d = x.shape
    row = lambda width: pl.BlockSpec((block_rows, width), lambda i: (i, 0))
    return pl.pallas_call(
        _out_proj_kernel,
        grid=(n // block_rows,),
        in_specs=[row(A_WIDTH), row(B_WIDTH), row(C_WIDTH), row(d),
                  pl.BlockSpec(w_out_bf16.shape, lambda i: (0, 0)),
                  pl.BlockSpec((1, d), lambda i: (0, 0))],
        out_specs=row(d),
        out_shape=jax.ShapeDtypeStruct((n, d), F32),
        compiler_params=pltpu.CompilerParams(dimension_semantics=("parallel",),
                                             vmem_limit_bytes=VMEM_LIMIT),
        name="out_proj",
    )(o_a, o_b, o_c, x, w_out_bf16, g_post.reshape(1, d))


def _split_halves(q):
    lane = lax.broadcasted_iota(jnp.int32, q.shape, 1)
    zero = jnp.zeros_like(q)
    return jnp.concatenate([jnp.where(lane < LANES // 2, q, zero), jnp.where(lane >= LANES // 2, q, zero)], axis=0)


def _lambda(lq1_ref, lk1_ref, lq2_ref, lk2_ref, lam_init):
    s1 = jnp.sum(lq1_ref[...] * lk1_ref[...], axis=-1, keepdims=True)
    s2 = jnp.sum(lq2_ref[...] * lk2_ref[...], axis=-1, keepdims=True)
    return jnp.exp(s1) - jnp.exp(s2) + lam_init


def _diff_finish(acc, l, lam, subln_g, gate, lam_init):
    r = acc.shape[0] // 2
    o = acc[:r] / l[:r] - lam * (acc[r:] / l[r:])
    o = o * lax.rsqrt(jnp.mean(o * o, axis=-1, keepdims=True) + EPS) * subln_g * (1.0 - lam_init)
    return o * gate


def _softmax_update(s, m_prev, l_prev, acc_prev, v_blocks):
    m_new = jnp.maximum(m_prev, jnp.max(s, axis=-1, keepdims=True))
    alpha = jnp.exp2(m_prev - m_new)
    p = jnp.exp2(s - _lane_tile(m_new, s.shape[1]))
    pb = p.astype(BF16)
    pv, start = None, 0
    for v in v_blocks:
        part = _dot(pb[:, start:start + v.shape[0]], v)
        pv = part if pv is None else pv + part
        start += v.shape[0]
    if pv.shape[1] == 2 * LANES:
        row_sum, pv = pv[:, LANES:], pv[:, :LANES]
    else:
        row_sum = jnp.sum(p, axis=-1, keepdims=True)
    return m_new, alpha * l_prev + row_sum, alpha * acc_prev + pv


def _with_ones(v):
    return jnp.concatenate([v, jnp.ones(v.shape, v.dtype)], axis=1)


def _triangle_tables(n_blocks, descending):
    qs, ks = [], []
    for qi in range(n_blocks):
        order = range(qi, -1, -1) if descending else range(qi + 1)
        for ki in order:
            qs.append(qi)
            ks.append(ki)
    return jnp.asarray(qs, jnp.int32), jnp.asarray(ks, jnp.int32)


def _diff_attn_kernel(qtab_ref, ktab_ref, q_ref, k_ref, v_ref, gate_ref, sub_ref,
                      lq1_ref, lk1_ref, lq2_ref, lk2_ref, o_ref,
                      qq_sc, m_sc, l_sc, acc_sc, *, lam_init, row_chunk):
    s_idx = pl.program_id(2)
    qi = qtab_ref[s_idx]
    ki = ktab_ref[s_idx]
    tq = q_ref.shape[0]
    tk = k_ref.shape[0]

    @pl.when(ki == 0)
    def _init():
        qq_sc[...] = _split_halves(q_ref[...])
        m_sc[...] = jnp.full(m_sc.shape, NEG_BIG, F32)
        l_sc[...] = jnp.zeros(l_sc.shape, F32)
        acc_sc[...] = jnp.zeros(acc_sc.shape, F32)

    def step(masked):
        k = k_ref[...]
        v_ones = _with_ones(v_ref[...])
        chunks = []
        for c in range(2 * tq // row_chunk):
            rows = slice(c * row_chunk, (c + 1) * row_chunk)
            q0 = (c * row_chunk) % tq
            width = min(tk, q0 + row_chunk) if masked else tk
            s = _dot_nt(qq_sc[rows], k[:width])
            if masked:
                qpos = q0 + lax.broadcasted_iota(jnp.int32, s.shape, 0)
                kpos = lax.broadcasted_iota(jnp.int32, s.shape, 1)
                s = jnp.where(kpos <= qpos, s, NEG_BIG)
            chunks.append((rows, width, s))
        weights = []
        for rows, width, s in chunks:
            m_prev = m_sc[rows]
            m_new = jnp.maximum(m_prev, jnp.max(s, axis=-1, keepdims=True))
            m_sc[rows] = m_new
            weights.append((jnp.exp2(m_prev - m_new), jnp.exp2(s - _lane_tile(m_new, width)).astype(BF16)))
        for (rows, width, _), (alpha, pb) in zip(chunks, weights):
            pv = _dot(pb, v_ones[:width])
            l_sc[rows] = alpha * l_sc[rows] + pv[:, LANES:]
            acc_sc[rows] = alpha * acc_sc[rows] + pv[:, :LANES]

    @pl.when(ki < qi)
    def _off_diag():
        step(False)

    @pl.when(ki == qi)
    def _diag():
        step(True)
        lam = _lambda(lq1_ref, lk1_ref, lq2_ref, lk2_ref, lam_init)
        o = _diff_finish(acc_sc[...], l_sc[...], lam, sub_ref[...], gate_ref[...].astype(F32), lam_init)
        o_ref[...] = o.astype(o_ref.dtype)


def _diff_attn(q, k, v, gate, subln_g, lams, *, batch, seq_len, block, row_chunk, lam_init):
    n = q.shape[0]
    nb = seq_len // block
    qtab, ktab = _triangle_tables(nb, descending=False)
    qmap = lambda b, h, s, qt, kt: (b * nb + qt[s], h)
    kmap = lambda b, h, s, qt, kt: (b * nb + kt[s], h)
    cmap = lambda b, h, s, qt, kt: (0, 0)
    blk = lambda m: pl.BlockSpec((block, LANES), m)
    lam_spec = pl.BlockSpec((1, A_DHEAD), cmap)
    stat = pltpu.VMEM((2 * block, LANES), F32)
    grid_spec = pltpu.PrefetchScalarGridSpec(
        num_scalar_prefetch=2,
        grid=(batch, A_HEADS, int(qtab.shape[0])),
        in_specs=[blk(qmap), blk(kmap), blk(kmap), blk(qmap), pl.BlockSpec((1, LANES), cmap),
                  lam_spec, lam_spec, lam_spec, lam_spec],
        out_specs=blk(qmap),
        scratch_shapes=[pltpu.VMEM((2 * block, LANES), BF16), stat, stat, stat])
    return pl.pallas_call(
        functools.partial(_diff_attn_kernel, lam_init=lam_init, row_chunk=row_chunk),
        grid_spec=grid_spec,
        out_shape=jax.ShapeDtypeStruct((n, A_WIDTH), BF16),
        compiler_params=pltpu.CompilerParams(dimension_semantics=("parallel", "parallel", "arbitrary"),
                                             vmem_limit_bytes=VMEM_LIMIT),
        name="diff_attn",
    )(qtab, ktab, q, k, v, gate, subln_g.reshape(1, LANES), *[a.reshape(1, A_DHEAD) for a in lams])


def _neg_later(n, with_total):
    j = lax.broadcasted_iota(jnp.int32, (n, n), 0)
    k = lax.broadcasted_iota(jnp.int32, (n, n), 1)
    m = jnp.where(j > k, -1.0, 0.0).astype(BF16)
    return jnp.concatenate([m, jnp.full((n, LANES), -1.0, BF16)], axis=1) if with_total else m


def _sb_attn_kernel(qtab_ref, ktab_ref, q_ref, k_ref, v_ref, gate_ref, o_ref,
                    qq_sc, carry_sc, acc_sc, *, sub, row_chunk):
    s_idx = pl.program_id(2)
    qi = qtab_ref[s_idx]
    ki = ktab_ref[s_idx]
    tq = q_ref.shape[0]
    tk = k_ref.shape[0]

    @pl.when(ki == qi)
    def _init():
        qq_sc[...] = _split_halves(q_ref[...])
        carry_sc[...] = jnp.zeros(carry_sc.shape, F32)
        acc_sc[...] = jnp.zeros(acc_sc.shape, F32)

    def step(masked):
        k = k_ref[...]
        v = v_ref[...]
        tri = _neg_later(sub, with_total=False)
        chunks = []
        for c in range(2 * tq // row_chunk):
            rows = slice(c * row_chunk, (c + 1) * row_chunk)
            q0 = (c * row_chunk) % tq
            width = min(tk, -(-(q0 + row_chunk) // sub) * sub) if masked else tk
            z = _dot_nt(qq_sc[rows], k[:width])
            valid = None
            if masked:
                qpos = q0 + lax.broadcasted_iota(jnp.int32, z.shape, 0)
                kpos = lax.broadcasted_iota(jnp.int32, z.shape, 1)
                valid = kpos < qpos
            chunks.append((rows, width, z, valid))
        prefix = []
        for rows, width, z, valid in chunks:
            sp = _softplus2(z)
            if valid is not None:
                sp = jnp.where(valid, sp, 0.0)
            later = [_dot(sp[:, j * sub:(j + 1) * sub].astype(BF16), tri) for j in range(width // sub)]
            prefix.append((sp, later))
        weights = []
        for (rows, width, z, valid), (sp, later) in zip(chunks, prefix):
            carry = carry_sc[rows]
            ws = [None] * (width // sub)
            for j in range(width // sub - 1, -1, -1):
                cols = slice(j * sub, (j + 1) * sub)
                t = z[:, cols] - sp[:, cols] + later[j] - _lane_tile(carry, sub)
                if valid is not None:
                    t = jnp.where(valid[:, cols], t, NEG_BIG)
                ws[j] = jnp.exp2(t)
                carry = carry + jnp.sum(sp[:, cols], axis=-1, keepdims=True)
            carry_sc[rows] = carry
            weights.append(jnp.concatenate(ws, axis=1).astype(BF16))
        for (rows, width, _, _), wb in zip(chunks, weights):
            acc_sc[rows] += _dot(wb, v[:width])

    @pl.when(ki == qi)
    def _diag():
        step(True)

    @pl.when(ki < qi)
    def _off_diag():
        step(False)

    @pl.when(ki == 0)
    def _finish():
        acc = acc_sc[...]
        lane = lax.broadcasted_iota(jnp.int32, (tq, LANES), 1)
        o = jnp.where(lane < LANES // 2, acc[:tq], acc[tq:])
        o_ref[...] = (o * gate_ref[...].astype(F32)).astype(o_ref.dtype)


def _sb_attn(q, k, v, gate, *, batch, seq_len, block, sub, row_chunk):
    n = q.shape[0]
    nb = seq_len // block
    qtab, ktab = _triangle_tables(nb, descending=True)
    qmap = lambda b, h, s, qt, kt: (b * nb + qt[s], h)
    kmap = lambda b, h, s, qt, kt: (b * nb + kt[s], h)
    blk = lambda m: pl.BlockSpec((block, LANES), m)
    grid_spec = pltpu.PrefetchScalarGridSpec(
        num_scalar_prefetch=2,
        grid=(batch, C_HEADS // 2, int(qtab.shape[0])),
        in_specs=[blk(qmap), blk(kmap), blk(kmap), blk(qmap)],
        out_specs=blk(qmap),
        scratch_shapes=[pltpu.VMEM((2 * block, LANES), BF16), pltpu.VMEM((2 * block, LANES), F32),
                        pltpu.VMEM((2 * block, LANES), F32)])
    return pl.pallas_call(
        functools.partial(_sb_attn_kernel, sub=sub, row_chunk=row_chunk),
        grid_spec=grid_spec,
        out_shape=jax.ShapeDtypeStruct((n, C_WIDTH), BF16),
        compiler_params=pltpu.CompilerParams(dimension_semantics=("parallel", "parallel", "arbitrary"),
                                             vmem_limit_bytes=VMEM_LIMIT),
        name="sb_attn",
    )(qtab, ktab, q, k, v, gate)


def _decode_kernel(pt_ref, qa_ref, kan_ref, van_ref, ga_ref, qc_ref, kcn_ref, vcn_ref, gc_ref, sub_ref,
                   lq1_ref, lk1_ref, lq2_ref, lk2_ref, *rest, pages_per_step, lam_init):
    pps = pages_per_step
    dk_refs = rest[0:pps]
    dv_refs = rest[pps:2 * pps]
    sk_refs = rest[2 * pps:3 * pps]
    sv_refs = rest[3 * pps:4 * pps]
    oa_ref, oc_ref = rest[4 * pps:4 * pps + 2]
    qqa_sc, qqc_sc, m_sc, l_sc, acca_sc, carry_sc, accc_sc = rest[4 * pps + 2:]
    j = pl.program_id(1)
    t = qa_ref.shape[0]
    page = sk_refs[0].shape[1]
    tri = _neg_later(page, with_total=True)

    def sb_page(z, valid, carry):
        sp = _softplus2(z)
        if valid is not None:
            sp = jnp.where(valid, sp, 0.0)
        later = _dot(sp.astype(BF16), tri)
        tt = z - sp + later[:, :page] - carry
        if valid is not None:
            tt = jnp.where(valid, tt, NEG_BIG)
        return jnp.exp2(tt), carry - later[:, page:]

    n_pairs = C_HEADS // 2

    @pl.when(j == 0)
    def _first():
        qpos = lax.broadcasted_iota(jnp.int32, (2 * t, page), 0) % t
        kpos = lax.broadcasted_iota(jnp.int32, (2 * t, page), 1)
        kan = kan_ref[...]
        van = van_ref[...]
        for h in range(A_HEADS):
            cols = slice(h * LANES, (h + 1) * LANES)
            qq = _split_halves(qa_ref[:, cols])
            qqa_sc[h] = qq
            s = jnp.where(kpos <= qpos, _dot_nt(qq, kan[:, cols]), NEG_BIG)
            m_sc[h], l_sc[h], acca_sc[h] = _softmax_update(
                s, jnp.full((2 * t, LANES), NEG_BIG, F32), jnp.zeros((2 * t, LANES), F32),
                jnp.zeros((2 * t, LANES), F32), [_with_ones(van[:, cols])])
        kcn = kcn_ref[...]
        vcn = vcn_ref[...]
        for pr in range(C_HEADS // 2):
            cols = slice(pr * LANES, (pr + 1) * LANES)
            qq = _split_halves(qc_ref[:, cols])
            qqc_sc[pr] = qq
            w, carry_sc[pr] = sb_page(_dot_nt(qq, kcn[:, cols]), kpos < qpos, jnp.zeros((2 * t, LANES), F32))
            accc_sc[pr] = _dot(w.astype(BF16), vcn[:, cols])

    head_rows = [pl.ds(h, page, stride=A_HEADS) for h in range(A_HEADS)]
    pair_rows = [slice(pr * LANES, (pr + 1) * LANES) for pr in range(n_pairs)]
    s_heads = [jnp.concatenate([_dot_nt(qqa_sc[h], r[head_rows[h], :].astype(BF16)) for r in dk_refs], axis=1)
               for h in range(A_HEADS)]
    z_pages = [jnp.concatenate([_dot(qqc_sc[pr], r[pair_rows[pr], :].astype(BF16)) for pr in range(n_pairs)], axis=0)
               for r in sk_refs]
    sp_pages = [_softplus2(z) for z in z_pages]
    later_pages = [_dot(sp.astype(BF16), tri) for sp in sp_pages]

    alphas, pbs = [], []
    for h in range(A_HEADS):
        m_prev = m_sc[h]
        m_new = jnp.maximum(m_prev, jnp.max(s_heads[h], axis=-1, keepdims=True))
        alpha = jnp.exp2(m_prev - m_new)
        p = jnp.exp2(s_heads[h] - _lane_tile(m_new, pps * page))
        m_sc[h] = m_new
        l_sc[h] = alpha * l_sc[h] + jnp.sum(p, axis=-1, keepdims=True)
        alphas.append(alpha)
        pbs.append(p.astype(BF16))
    carry = jnp.concatenate([carry_sc[pr] for pr in range(n_pairs)], axis=0)
    wbs = []
    for i in range(pps):
        wbs.append(jnp.exp2(z_pages[i] - sp_pages[i] + later_pages[i][:, :page] - carry).astype(BF16))
        carry = carry - later_pages[i][:, page:]
    for pr in range(n_pairs):
        carry_sc[pr] = carry[pr * 2 * t:(pr + 1) * 2 * t]

    for h in range(A_HEADS):
        pv = _dot(pbs[h][:, :page], dv_refs[0][head_rows[h], :].astype(BF16))
        for i in range(1, pps):
            pv = pv + _dot(pbs[h][:, i * page:(i + 1) * page], dv_refs[i][head_rows[h], :].astype(BF16))
        acca_sc[h] = alphas[h] * acca_sc[h] + pv
    for pr in range(n_pairs):
        rows = slice(pr * 2 * t, (pr + 1) * 2 * t)
        pv = _dot_nt(wbs[0][rows], sv_refs[0][pair_rows[pr], :].astype(BF16))
        for i in range(1, pps):
            pv = pv + _dot_nt(wbs[i][rows], sv_refs[i][pair_rows[pr], :].astype(BF16))
        accc_sc[pr] += pv

    @pl.when(j == pl.num_programs(1) - 1)
    def _finish():
        lam = _lambda(lq1_ref, lk1_ref, lq2_ref, lk2_ref, lam_init)
        lane = lax.broadcasted_iota(jnp.int32, (t, LANES), 1)
        for h in range(A_HEADS):
            cols = slice(h * LANES, (h + 1) * LANES)
            o = _diff_finish(acca_sc[h], l_sc[h], lam, sub_ref[...], ga_ref[:, cols].astype(F32), lam_init)
            oa_ref[:, cols] = o.astype(oa_ref.dtype)
        for pr in range(C_HEADS // 2):
            cols = slice(pr * LANES, (pr + 1) * LANES)
            acc = accc_sc[pr]
            o = jnp.where(lane < LANES // 2, acc[:t], acc[t:])
            oc_ref[:, cols] = (o * gc_ref[:, cols].astype(F32)).astype(oc_ref.dtype)


def _decode(layer, page_table, cache_dk, cache_dv, cache_sk, cache_sv,
            q_a, k_a_new, v_a_new, gate_a, q_c, k_c_new, v_c_new, gate_c, subln_g, lams,
            *, dec_batch, dec_seq, pages_per_step, lam_init):
    n_pages = page_table.shape[1]
    depth, n_pool, page = cache_dk.shape[:3]
    pps = pages_per_step
    assert n_pages % pps == 0
    n_steps = n_pages // pps
    cdk = cache_dk.reshape(depth, n_pool, page * A_HEADS, 2 * A_DHEAD)
    cdv = cache_dv.reshape(depth, n_pool, page * A_HEADS, 2 * A_DHEAD)
    csk = jnp.transpose(cache_sk, (0, 1, 3, 4, 2)).reshape(depth, n_pool, C_WIDTH, page)
    csv = jnp.transpose(cache_sv, (0, 1, 3, 4, 2)).reshape(depth, n_pool, C_WIDTH, page)
    pt_flat = page_table.reshape(-1)

    per_seq = lambda a: a.reshape(dec_batch, dec_seq, a.shape[-1])
    pad_page = lambda a: jnp.pad(per_seq(a), ((0, 0), (0, page - dec_seq), (0, 0)))
    seq_spec = lambda rows, width: pl.BlockSpec((None, rows, width), lambda b, j, pt: (b, 0, 0))
    const_spec = lambda width: pl.BlockSpec((1, width), lambda b, j, pt: (0, 0))

    def page_spec(i, rows, width):
        def index_map(b, j, pt):
            return (layer, pt[b * n_pages + (n_pages - 1 - (j * pps + i))], 0, 0)
        return pl.BlockSpec((None, None, rows, width), index_map)

    in_specs = ([seq_spec(dec_seq, A_WIDTH), seq_spec(page, A_WIDTH), seq_spec(page, A_WIDTH),
                 seq_spec(dec_seq, A_WIDTH),
                 seq_spec(dec_seq, C_WIDTH), seq_spec(page, C_WIDTH), seq_spec(page, C_WIDTH),
                 seq_spec(dec_seq, C_WIDTH),
                 const_spec(LANES)] + [const_spec(A_DHEAD)] * 4
                + [page_spec(i, page * A_HEADS, 2 * A_DHEAD) for i in range(pps)] * 2
                + [page_spec(i, C_WIDTH, page) for i in range(pps)] * 2)
    t2 = 2 * dec_seq
    stat_a = pltpu.VMEM((A_HEADS, t2, LANES), F32)
    stat_c = pltpu.VMEM((C_HEADS // 2, t2, LANES), F32)
    grid_spec = pltpu.PrefetchScalarGridSpec(
        num_scalar_prefetch=1,
        grid=(dec_batch, n_steps),
        in_specs=in_specs,
        out_specs=[seq_spec(dec_seq, A_WIDTH), seq_spec(dec_seq, C_WIDTH)],
        scratch_shapes=[pltpu.VMEM((A_HEADS, t2, LANES), BF16), pltpu.VMEM((C_HEADS // 2, t2, LANES), BF16),
                        stat_a, stat_a, stat_a, stat_c, stat_c])
    o_a, o_c = pl.pallas_call(
        functools.partial(_decode_kernel, pages_per_step=pps, lam_init=lam_init),
        grid_spec=grid_spec,
        out_shape=[jax.ShapeDtypeStruct((dec_batch, dec_seq, A_WIDTH), F32),
                   jax.ShapeDtypeStruct((dec_batch, dec_seq, C_WIDTH), F32)],
        compiler_params=pltpu.CompilerParams(dimension_semantics=("parallel", "arbitrary"),
                                             vmem_limit_bytes=VMEM_LIMIT),
        name="decode_attn",
    )(pt_flat, per_seq(q_a), pad_page(k_a_new), pad_page(v_a_new), per_seq(gate_a),
      per_seq(q_c), pad_page(k_c_new), pad_page(v_c_new), per_seq(gate_c),
      subln_g.reshape(1, LANES), *[a.reshape(1, A_DHEAD) for a in lams],
      *([cdk] * pps), *([cdv] * pps), *([csk] * pps), *([csv] * pps))
    n = dec_batch * dec_seq
    return o_a.reshape(n, A_WIDTH).astype(BF16), o_c.reshape(n, C_WIDTH).astype(BF16)


PROMPT_ROWS = 512
ATTN_BLOCK = 512
ATTN_ROW_CHUNK = 256
SB_SUB = 256
PAGES_PER_STEP = 8


def kernel(x_prompt, x_sample, cache_diff_k, cache_diff_v, cache_sb_k, cache_sb_v, page_table, w_in, w_out, norm_pre_g, norm_post_g, lambda_q1, lambda_k1, lambda_q2, lambda_k2, diff_subln_g, sgu_ln_g, sgu_ln_b, sgu_w, sgu_b):
    batch, seq, d = x_prompt.shape
    dec_batch, dec_seq, _ = x_sample.shape
    depth = w_in.shape[0]
    w_in_b = w_in.astype(BF16)
    w_out_b = w_out.astype(BF16)
    hp = x_prompt.reshape(batch * seq, d)
    hs = x_sample.reshape(dec_batch * dec_seq, d)
    block = min(ATTN_BLOCK, seq)
    sub = min(SB_SUB, block)
    row_chunk = min(ATTN_ROW_CHUNK, block)
    prompt_rows = min(PROMPT_ROWS, seq)
    outs_p = [[] for _ in range(4)]
    outs_s = [[] for _ in range(5)]
    for l in range(depth):
        lam_init = 0.8 - 0.6 * math.exp(-0.3 * l)
        lams = (lambda_q1[l], lambda_k1[l], lambda_q2[l], lambda_k2[l])

        (ak, av, ck, cv, _, qa, ka, va, ga, ob, qc, kc, vc, gc) = _in_proj(
            hp, norm_pre_g[l], w_in_b[l], sgu_ln_g[l], sgu_ln_b[l], sgu_w[l], sgu_b[l],
            seq_len=seq, block_rows=prompt_rows)
        oa = _diff_attn(qa, ka, va, ga, diff_subln_g[l], lams, batch=batch, seq_len=seq, block=block,
                        row_chunk=row_chunk, lam_init=lam_init)
        oc = _sb_attn(qc, kc, vc, gc, batch=batch, seq_len=seq, block=block, sub=sub, row_chunk=row_chunk)
        hp = _out_proj(oa, ob, oc, hp, w_out_b[l], norm_post_g[l], block_rows=prompt_rows)
        for dst, a, heads in zip(outs_p, (ak, av, ck, cv), (A_HEADS, A_HEADS, C_HEADS, C_HEADS)):
            dst.append(a.reshape(batch, seq, heads, a.shape[-1] // heads))

        (ak, av, ck, cv, vn, qa, ka, va, ga, ob, qc, kc, vc, gc) = _in_proj(
            hs, norm_pre_g[l], w_in_b[l], sgu_ln_g[l], sgu_ln_b[l], sgu_w[l], sgu_b[l],
            seq_len=dec_seq, block_rows=dec_batch * dec_seq)
        oa, oc = _decode(l, page_table, cache_diff_k, cache_diff_v, cache_sb_k, cache_sb_v,
                         qa, ka, va, ga, qc, kc, vc, gc, diff_subln_g[l], lams,
                         dec_batch=dec_batch, dec_seq=dec_seq, pages_per_step=PAGES_PER_STEP,
                         lam_init=lam_init)
        hs = _out_proj(oa, ob, oc, hs, w_out_b[l], norm_post_g[l], block_rows=dec_batch * dec_seq)
        for dst, a, heads in zip(outs_s, (ak, av, ck, cv), (A_HEADS, A_HEADS, C_HEADS, C_HEADS)):
            dst.append(a.reshape(dec_batch, dec_seq, heads, a.shape[-1] // heads))
        outs_s[4].append(vn.reshape(dec_batch, dec_seq, B_WIDTH))

    stack = lambda xs: jnp.stack(xs, 0)
    return (hp.reshape(batch, seq, d), hs.reshape(dec_batch, dec_seq, d),
            *[stack(o) for o in outs_p], *[stack(o) for o in outs_s])
```

```python
import functools
import math

import jax
import jax.numpy as jnp
from jax import lax
from jax.experimental import pallas as pl
from jax.experimental.pallas import tpu as pltpu

A_HEADS = 4
A_DHEAD = 64
A_WIDTH = A_HEADS * 2 * A_DHEAD
B_GROUPS = 4
B_DGROUP = 64
B_WIDTH = B_GROUPS * B_DGROUP
CHUNK = 128
C_HEADS = 4
C_DHEAD = 64
C_WIDTH = C_HEADS * C_DHEAD
IN_COLS = 4 * A_WIDTH + 3 * B_WIDTH + 4 * C_WIDTH
EPS = 1e-6
LANES = 128
NEG_BIG = -1e30
LOG2E = 1.4426950408889634
VMEM_LIMIT = 56 * 1024 * 1024

_OFF = {}
_o = 0
for _name, _w in (("a_q", A_WIDTH), ("a_k", A_WIDTH), ("a_v", A_WIDTH), ("a_g", A_WIDTH),
                  ("b_u", B_WIDTH), ("b_v", B_WIDTH), ("b_g", B_WIDTH),
                  ("c_q", C_WIDTH), ("c_k", C_WIDTH), ("c_v", C_WIDTH), ("c_g", C_WIDTH)):
    _OFF[_name] = (_o, _w)
    _o += _w

BF16 = jnp.bfloat16
F32 = jnp.float32


def _silu(x):
    return x / (1.0 + jnp.exp(-x))


def _softplus2(z):
    neg_abs = lax.bitcast_convert_type(lax.bitcast_convert_type(z, jnp.uint32) | jnp.uint32(0x80000000), F32)
    return jnp.maximum(z, 0.0) + jnp.log2(1.0 + jnp.exp2(neg_abs))


def _lane_tile(x, width):
    return x if width == LANES else jnp.concatenate([x] * (width // LANES), axis=1)


def _dot_nt(a, b):
    return lax.dot_general(a, b, (((1,), (1,)), ((), ())), preferred_element_type=F32)


def _dot(a, b):
    return jnp.dot(a, b, preferred_element_type=F32)


def _in_proj_kernel(x_ref, gpre_ref, w_ref, lng_ref, lnb_ref, wt_ref, bias_ref, *rest,
                    chunk_len, mm_rows, cache_layout, n_aliased):
    if cache_layout:
        ak_ref, av_ref, ck_ref, cv_ref = rest[n_aliased:4 + n_aliased]
        vn_ref = None
        qa_ref, ka_ref, va_ref, ga_ref, ob_ref, qc_ref, kc_ref, vc_ref, gc_ref = rest[4 + n_aliased:]
    else:
        ak_ref, av_ref, ck_ref, cv_ref, vn_ref = rest[:5]
        qa_ref, ka_ref, va_ref, ga_ref, ob_ref, qc_ref, kc_ref, vc_ref, gc_ref = rest[5:]
    tm = x_ref.shape[0]
    x = x_ref[...]
    h = x * lax.rsqrt(jnp.mean(x * x, axis=-1, keepdims=True) + EPS) * gpre_ref[...]
    hb = h.astype(BF16)

    def proj(name):
        off, width = _OFF[name]
        return _dot(hb, w_ref[:, off:off + width])

    def store_a(ref, val):
        if cache_layout:
            for head in range(A_HEADS):
                ref[pl.ds(head, tm, stride=A_HEADS), :] = val[:, head * LANES:(head + 1) * LANES]
        else:
            ref[...] = val

    a_q = proj("a_q")
    qa_ref[...] = (a_q * (A_DHEAD ** -0.5 * LOG2E)).astype(BF16)
    a_k = proj("a_k")
    store_a(ak_ref, a_k)
    ka_ref[...] = a_k.astype(BF16)
    a_v = proj("a_v")
    store_a(av_ref, a_v)
    va_ref[...] = a_v.astype(BF16)
    ga_ref[...] = _silu(proj("a_g")).astype(BF16)
    c_q = proj("c_q")
    qc_ref[...] = (c_q * (C_DHEAD ** -0.5 * LOG2E)).astype(BF16)
    c_k = proj("c_k")
    kc_ref[...] = c_k.astype(BF16)
    c_v = proj("c_v")
    vc_ref[...] = c_v.astype(BF16)
    if cache_layout:
        ck_ref[...] = c_k.T
        cv_ref[...] = c_v.T
    else:
        ck_ref[...] = c_k
        cv_ref[...] = c_v
    gc_ref[...] = _silu(proj("c_g")).astype(BF16)

    b_v = proj("b_v")
    mu = jnp.mean(b_v, axis=-1, keepdims=True)
    cen = b_v - mu
    var = jnp.mean(cen * cen, axis=-1, keepdims=True)
    vn = cen * lax.rsqrt(var + EPS) * lng_ref[...] + lnb_ref[...]
    if vn_ref is not None:
        vn_ref[...] = vn
    vnb = vn.astype(BF16)

    r = lax.broadcasted_iota(jnp.int32, (B_GROUPS * mm_rows, mm_rows), 0) % mm_rows
    c = lax.broadcasted_iota(jnp.int32, (B_GROUPS * mm_rows, mm_rows), 1)
    keep = (r // chunk_len == c // chunk_len) & (c % chunk_len <= r % chunk_len)
    wt = jnp.where(keep, wt_ref[...], 0.0).astype(BF16)
    grp = lax.broadcasted_iota(jnp.int32, (mm_rows, B_WIDTH), 1) // B_DGROUP
    b_u = proj("b_u")
    gate_b = _silu(proj("b_g"))
    for i in range(tm // mm_rows):
        rows = slice(i * mm_rows, (i + 1) * mm_rows)
        full = _dot(wt, vnb[rows])
        mixed = bias_ref[...]
        for g in range(B_GROUPS):
            mixed = mixed + jnp.where(grp == g, full[g * mm_rows:(g + 1) * mm_rows], 0.0)
        ob_ref[rows, :] = (b_u[rows] * mixed * gate_b[rows]).astype(BF16)


def _in_proj(x, g_pre, w_in_bf16, ln_g, ln_b, sgu_w, sgu_b, *, seq_len, block_rows, stacked=None):
    n, d = x.shape
    chunk_len = min(seq_len, CHUNK)
    mm_rows = CHUNK if chunk_len == CHUNK else block_rows
    assert n % block_rows == 0 and block_rows % mm_rows == 0 and mm_rows % chunk_len == 0
    assert seq_len % chunk_len == 0
    rep = mm_rows // chunk_len
    wt = jnp.tile(sgu_w[:, :chunk_len, :chunk_len], (1, rep, rep)).reshape(B_GROUPS * mm_rows, mm_rows)
    bias = jnp.repeat(jnp.tile(sgu_b[:, :chunk_len], (1, rep)).T, B_DGROUP, axis=1)

    row = lambda width: pl.BlockSpec((block_rows, width), lambda i: (i, 0))
    const = lambda shape: pl.BlockSpec(shape, lambda i: (0, 0))
    operands = [x, g_pre.reshape(1, d), w_in_bf16, ln_g.reshape(1, B_WIDTH), ln_b.reshape(1, B_WIDTH), wt, bias]
    in_specs = [row(d), const((1, d)), const((d, IN_COLS)), const((1, B_WIDTH)), const((1, B_WIDTH)),
                const((B_GROUPS * mm_rows, mm_rows)), const((mm_rows, B_WIDTH))]
    bf16_shapes = [jax.ShapeDtypeStruct((n, wd), BF16)
                   for wd in (A_WIDTH, A_WIDTH, A_WIDTH, A_WIDTH, B_WIDTH, C_WIDTH, C_WIDTH, C_WIDTH, C_WIDTH)]
    aliases = {}
    if stacked is None:
        f32_shapes = [jax.ShapeDtypeStruct((n, wd), F32) for wd in (A_WIDTH, A_WIDTH, C_WIDTH, C_WIDTH, B_WIDTH)]
        f32_specs = [row(s.shape[1]) for s in f32_shapes]
        n_aliased = 0
    else:
        layer, depth, previous = stacked
        assert seq_len % block_rows == 0
        batch, per_seq = n // seq_len, seq_len // block_rows
        f32_shapes = ([jax.ShapeDtypeStruct((depth, n * A_HEADS, 2 * A_DHEAD), F32)] * 2
                      + [jax.ShapeDtypeStruct((depth, batch, C_WIDTH, seq_len), F32)] * 2)
        f32_specs = ([pl.BlockSpec((None, block_rows * A_HEADS, 2 * A_DHEAD), lambda i: (layer, i, 0))] * 2
                     + [pl.BlockSpec((None, None, C_WIDTH, block_rows),
                                     lambda i: (layer, i // per_seq, 0, i % per_seq))] * 2)
        n_aliased = 0 if previous is None else len(previous)
        for i in range(n_aliased):
            aliases[len(operands)] = i
            operands.append(previous[i])
            in_specs.append(pl.BlockSpec(memory_space=pl.ANY))
    out_shapes = f32_shapes + bf16_shapes
    outs = pl.pallas_call(
        functools.partial(_in_proj_kernel, chunk_len=chunk_len, mm_rows=mm_rows,
                          cache_layout=stacked is not None, n_aliased=n_aliased),
        grid=(n // block_rows,),
        in_specs=in_specs,
        out_specs=f32_specs + [row(s.shape[1]) for s in bf16_shapes],
        out_shape=out_shapes,
        input_output_aliases=aliases,
        compiler_params=pltpu.CompilerParams(dimension_semantics=("parallel",),
                                             vmem_limit_bytes=VMEM_LIMIT),
        name="in_proj",
    )(*operands)
    if stacked is not None:
        outs = list(outs[:4]) + [None] + list(outs[4:])
    return tuple(outs)


def _out_proj_kernel(oa_ref, ob_ref, oc_ref, x_ref, w_ref, g_ref, y_ref):
    mix = (_dot(oa_ref[...], w_ref[0:A_WIDTH, :])
           + _dot(ob_ref[...], w_ref[A_WIDTH:A_WIDTH + B_WIDTH, :])
           + _dot(oc_ref[...], w_ref[A_WIDTH + B_WIDTH:, :]))
    y_ref[...] = x_ref[...] + mix * lax.rsqrt(jnp.mean(mix * mix, axis=-1, keepdims=True) + EPS) * g_ref[...]


def _out_proj(o_a, o_b, o_c, x, w_out_bf16, g_post, *, block_rows):
    n, d = x.shape
    row = lambda width: pl.BlockSpec((block_rows, width), lambda i: (i, 0))
    return pl.pallas_call(
        _out_proj_kernel,
        grid=(n // block_rows,),
        in_specs=[row(A_WIDTH), row(B_WIDTH), row(C_WIDTH), row(d),
                  pl.BlockSpec(w_out_bf16.shape, lambda i: (0, 0)),
                  pl.BlockSpec((1, d), lambda i: (0, 0))],
        out_specs=row(d),
        out_shape=jax.ShapeDtypeStruct((n, d), F32),
        compiler_params=pltpu.CompilerParams(dimension_semantics=("parallel",),
                                             vmem_limit_bytes=VMEM_LIMIT),
        name="out_proj",
    )(o_a, o_b, o_c, x, w_out_bf16, g_post.reshape(1, d))


def _split_halves(q):
    lane = lax.broadcasted_iota(jnp.int32, q.shape, 1)
    zero = jnp.zeros_like(q)
    return jnp.concatenate([jnp.where(lane < LANES // 2, q, zero), jnp.where(lane >= LANES // 2, q, zero)], axis=0)


def _lambda(lq1_ref, lk1_ref, lq2_ref, lk2_ref, lam_init):
    s1 = jnp.sum(lq1_ref[...] * lk1_ref[...], axis=-1, keepdims=True)
    s2 = jnp.sum(lq2_ref[...] * lk2_ref[...], axis=-1, keepdims=True)
    return jnp.exp(s1) - jnp.exp(s2) + lam_init


def _diff_finish(acc, l, lam, subln_g, gate, lam_init):
    r = acc.shape[0] // 2
    o = acc[:r] / l[:r] - lam * (acc[r:] / l[r:])
    o = o * lax.rsqrt(jnp.mean(o * o, axis=-1, keepdims=True) + EPS) * subln_g * (1.0 - lam_init)
    return o * gate


def _softmax_update(s, m_prev, l_prev, acc_prev, v_blocks):
    m_new = jnp.maximum(m_prev, jnp.max(s, axis=-1, keepdims=True))
    alpha = jnp.exp2(m_prev - m_new)
    p = jnp.exp2(s - _lane_tile(m_new, s.shape[1]))
    pb = p.astype(BF16)
    pv, start = None, 0
    for v in v_blocks:
        part = _dot(pb[:, start:start + v.shape[0]], v)
        pv = part if pv is None else pv + part
        start += v.shape[0]
    if pv.shape[1] == 2 * LANES:
        row_sum, pv = pv[:, LANES:], pv[:, :LANES]
    else:
        row_sum = jnp.sum(p, axis=-1, keepdims=True)
    return m_new, alpha * l_prev + row_sum, alpha * acc_prev + pv


def _with_ones(v):
    return jnp.concatenate([v, jnp.ones(v.shape, v.dtype)], axis=1)


def _triangle_tables(n_blocks, descending):
    qs, ks = [], []
    for qi in range(n_blocks):
        order = range(qi, -1, -1) if descending else range(qi + 1)
        for ki in order:
            qs.append(qi)
            ks.append(ki)
    return jnp.asarray(qs, jnp.int32), jnp.asarray(ks, jnp.int32)


def _diff_attn_kernel(qtab_ref, ktab_ref, q_ref, k_ref, v_ref, gate_ref, sub_ref,
                      lq1_ref, lk1_ref, lq2_ref, lk2_ref, o_ref,
                      qq_sc, m_sc, l_sc, acc_sc, *, lam_init, row_chunk):
    s_idx = pl.program_id(2)
    qi = qtab_ref[s_idx]
    ki = ktab_ref[s_idx]
    tq = q_ref.shape[0]
    tk = k_ref.shape[0]

    @pl.when(ki == 0)
    def _init():
        qq_sc[...] = _split_halves(q_ref[...])
        m_sc[...] = jnp.full(m_sc.shape, NEG_BIG, F32)
        l_sc[...] = jnp.zeros(l_sc.shape, F32)
        acc_sc[...] = jnp.zeros(acc_sc.shape, F32)

    def step(masked):
        k = k_ref[...]
        v_ones = _with_ones(v_ref[...])
        chunks = []
        for c in range(2 * tq // row_chunk):
            rows = slice(c * row_chunk, (c + 1) * row_chunk)
            q0 = (c * row_chunk) % tq
            width = min(tk, q0 + row_chunk) if masked else tk
            s = _dot_nt(qq_sc[rows], k[:width])
            if masked:
                qpos = q0 + lax.broadcasted_iota(jnp.int32, s.shape, 0)
                kpos = lax.broadcasted_iota(jnp.int32, s.shape, 1)
                s = jnp.where(kpos <= qpos, s, NEG_BIG)
            chunks.append((rows, width, s))
        weights = []
        for rows, width, s in chunks:
            m_prev = m_sc[rows]
            m_new = jnp.maximum(m_prev, jnp.max(s, axis=-1, keepdims=True))
            m_sc[rows] = m_new
            weights.append((jnp.exp2(m_prev - m_new), jnp.exp2(s - _lane_tile(m_new, width)).astype(BF16)))
        for (rows, width, _), (alpha, pb) in zip(chunks, weights):
            pv = _dot(pb, v_ones[:width])
            l_sc[rows] = alpha * l_sc[rows] + pv[:, LANES:]
            acc_sc[rows] = alpha * acc_sc[rows] + pv[:, :LANES]

    @pl.when(ki < qi)
    def _off_diag():
        step(False)

    @pl.when(ki == qi)
    def _diag():
        step(True)
        lam = _lambda(lq1_ref, lk1_ref, lq2_ref, lk2_ref, lam_init)
        o = _diff_finish(acc_sc[...], l_sc[...], lam, sub_ref[...], gate_ref[...].astype(F32), lam_init)
        o_ref[...] = o.astype(o_ref.dtype)


def _diff_attn(q, k, v, gate, subln_g, lams, *, batch, seq_len, block, row_chunk, lam_init):
    n = q.shape[0]
    nb = seq_len // block
    qtab, ktab = _triangle_tables(nb, descending=False)
    qmap = lambda b, h, s, qt, kt: (b * nb + qt[s], h)
    kmap = lambda b, h, s, qt, kt: (b * nb + kt[s], h)
    cmap = lambda b, h, s, qt, kt: (0, 0)
    blk = lambda m: pl.BlockSpec((block, LANES), m)
    lam_spec = pl.BlockSpec((1, A_DHEAD), cmap)
    stat = pltpu.VMEM((2 * block, LANES), F32)
    grid_spec = pltpu.PrefetchScalarGridSpec(
        num_scalar_prefetch=2,
        grid=(batch, A_HEADS, int(qtab.shape[0])),
        in_specs=[blk(qmap), blk(kmap), blk(kmap), blk(qmap), pl.BlockSpec((1, LANES), cmap),
                  lam_spec, lam_spec, lam_spec, lam_spec],
        out_specs=blk(qmap),
        scratch_shapes=[pltpu.VMEM((2 * block, LANES), BF16), stat, stat, stat])
    return pl.pallas_call(
        functools.partial(_diff_attn_kernel, lam_init=lam_init, row_chunk=row_chunk),
        grid_spec=grid_spec,
        out_shape=jax.ShapeDtypeStruct((n, A_WIDTH), BF16),
        compiler_params=pltpu.CompilerParams(dimension_semantics=("parallel", "parallel", "arbitrary"),
                                             vmem_limit_bytes=VMEM_LIMIT),
        name="diff_attn",
    )(qtab, ktab, q, k, v, gate, subln_g.reshape(1, LANES), *[a.reshape(1, A_DHEAD) for a in lams])


def _neg_later(n, with_total):
    j = lax.broadcasted_iota(jnp.int32, (n, n), 0)
    k = lax.broadcasted_iota(jnp.int32, (n, n), 1)
    m = jnp.where(j > k, -1.0, 0.0).astype(BF16)
    return jnp.concatenate([m, jnp.full((n, LANES), -1.0, BF16)], axis=1) if with_total else m


def _sb_attn_kernel(qtab_ref, ktab_ref, q_ref, k_ref, v_ref, gate_ref, o_ref,
                    qq_sc, carry_sc, acc_sc, *, sub, row_chunk):
    s_idx = pl.program_id(2)
    qi = qtab_ref[s_idx]
    ki = ktab_ref[s_idx]
    tq = q_ref.shape[0]
    tk = k_ref.shape[0]

    @pl.when(ki == qi)
    def _init():
        qq_sc[...] = _split_halves(q_ref[...])
        carry_sc[...] = jnp.zeros(carry_sc.shape, F32)
        acc_sc[...] = jnp.zeros(acc_sc.shape, F32)

    def step(masked):
        k = k_ref[...]
        v = v_ref[...]
        tri = _neg_later(sub, with_total=False)
        chunks = []
        for c in range(2 * tq // row_chunk):
            rows = slice(c * row_chunk, (c + 1) * row_chunk)
            q0 = (c * row_chunk) % tq
            width = min(tk, -(-(q0 + row_chunk) // sub) * sub) if masked else tk
            z = _dot_nt(qq_sc[rows], k[:width])
            valid = None
            if masked:
                qpos = q0 + lax.broadcasted_iota(jnp.int32, z.shape, 0)
                kpos = lax.broadcasted_iota(jnp.int32, z.shape, 1)
                valid = kpos < qpos
            chunks.append((rows, width, z, valid))
        prefix = []
        for rows, width, z, valid in chunks:
            sp = _softplus2(z)
            if valid is not None:
                sp = jnp.where(valid, sp, 0.0)
            later = [_dot(sp[:, j * sub:(j + 1) * sub].astype(BF16), tri) for j in range(width // sub)]
            prefix.append((sp, later))
        weights = []
        for (rows, width, z, valid), (sp, later) in zip(chunks, prefix):
            carry = carry_sc[rows]
            ws = [None] * (width // sub)
            for j in range(width // sub - 1, -1, -1):
                cols = slice(j * sub, (j + 1) * sub)
                t = z[:, cols] - sp[:, cols] + later[j] - _lane_tile(carry, sub)
                if valid is not None:
                    t = jnp.where(valid[:, cols], t, NEG_BIG)
                ws[j] = jnp.exp2(t)
                carry = carry + jnp.sum(sp[:, cols], axis=-1, keepdims=True)
            carry_sc[rows] = carry
            weights.append(jnp.concatenate(ws, axis=1).astype(BF16))
        for (rows, width, _, _), wb in zip(chunks, weights):
            acc_sc[rows] += _dot(wb, v[:width])

    @pl.when(ki == qi)
    def _diag():
        step(True)

    @pl.when(ki < qi)
    def _off_diag():
        step(False)

    @pl.when(ki == 0)
    def _finish():
        acc = acc_sc[...]
        lane = lax.broadcasted_iota(jnp.int32, (tq, LANES), 1)
        o = jnp.where(lane < LANES // 2, acc[:tq], acc[tq:])
        o_ref[...] = (o * gate_ref[...].astype(F32)).astype(o_ref.dtype)


def _sb_attn(q, k, v, gate, *, batch, seq_len, block, sub, row_chunk):
    n = q.shape[0]
    nb = seq_len // block
    qtab, ktab = _triangle_tables(nb, descending=True)
    qmap = lambda b, h, s, qt, kt: (b * nb + qt[s], h)
    kmap = lambda b, h, s, qt, kt: (b * nb + kt[s], h)
    blk = lambda m: pl.BlockSpec((block, LANES), m)
    grid_spec = pltpu.PrefetchScalarGridSpec(
        num_scalar_prefetch=2,
        grid=(batch, C_HEADS // 2, int(qtab.shape[0])),
        in_specs=[blk(qmap), blk(kmap), blk(kmap), blk(qmap)],
        out_specs=blk(qmap),
        scratch_shapes=[pltpu.VMEM((2 * block, LANES), BF16), pltpu.VMEM((2 * block, LANES), F32),
                        pltpu.VMEM((2 * block, LANES), F32)])
    return pl.pallas_call(
        functools.partial(_sb_attn_kernel, sub=sub, row_chunk=row_chunk),
        grid_spec=grid_spec,
        out_shape=jax.ShapeDtypeStruct((n, C_WIDTH), BF16),
        compiler_params=pltpu.CompilerParams(dimension_semantics=("parallel", "parallel", "arbitrary"),
                                             vmem_limit_bytes=VMEM_LIMIT),
        name="sb_attn",
    )(qtab, ktab, q, k, v, gate)


def _decode_kernel(pt_ref, qa_ref, kan_ref, van_ref, ga_ref, qc_ref, kcn_ref, vcn_ref, gc_ref, sub_ref,
                   lq1_ref, lk1_ref, lq2_ref, lk2_ref, *rest, pages_per_step, lam_init):
    pps = pages_per_step
    dk_refs = rest[0:pps]
    dv_refs = rest[pps:2 * pps]
    sk_refs = rest[2 * pps:3 * pps]
    sv_refs = rest[3 * pps:4 * pps]
    oa_ref, oc_ref = rest[4 * pps:4 * pps + 2]
    qqa_sc, qqc_sc, m_sc, l_sc, acca_sc, carry_sc, accc_sc = rest[4 * pps + 2:]
    j = pl.program_id(1)
    t = qa_ref.shape[0]
    page = sk_refs[0].shape[1]
    tri = _neg_later(page, with_total=True)

    def sb_page(z, valid, carry):
        sp = _softplus2(z)
        if valid is not None:
            sp = jnp.where(valid, sp, 0.0)
        later = _dot(sp.astype(BF16), tri)
        tt = z - sp + later[:, :page] - carry
        if valid is not None:
            tt = jnp.where(valid, tt, NEG_BIG)
        return jnp.exp2(tt), carry - later[:, page:]

    n_pairs = C_HEADS // 2

    @pl.when(j == 0)
    def _first():
        qpos = lax.broadcasted_iota(jnp.int32, (2 * t, page), 0) % t
        kpos = lax.broadcasted_iota(jnp.int32, (2 * t, page), 1)
        kan = kan_ref[...]
        van = van_ref[...]
        for h in range(A_HEADS):
            cols = slice(h * LANES, (h + 1) * LANES)
            qq = _split_halves(qa_ref[:, cols])
            qqa_sc[h] = qq
            s = jnp.where(kpos <= qpos, _dot_nt(qq, kan[:, cols]), NEG_BIG)
            m_sc[h], l_sc[h], acca_sc[h] = _softmax_update(
                s, jnp.full((2 * t, LANES), NEG_BIG, F32), jnp.zeros((2 * t, LANES), F32),
                jnp.zeros((2 * t, LANES), F32), [_with_ones(van[:, cols])])
        kcn = kcn_ref[...]
        vcn = vcn_ref[...]
        for pr in range(C_HEADS // 2):
            cols = slice(pr * LANES, (pr + 1) * LANES)
            qq = _split_halves(qc_ref[:, cols])
            qqc_sc[pr] = qq
            w, carry_sc[pr] = sb_page(_dot_nt(qq, kcn[:, cols]), kpos < qpos, jnp.zeros((2 * t, LANES), F32))
            accc_sc[pr] = _dot(w.astype(BF16), vcn[:, cols])

    head_rows = [pl.ds(h, page, stride=A_HEADS) for h in range(A_HEADS)]
    pair_rows = [slice(pr * LANES, (pr + 1) * LANES) for pr in range(n_pairs)]
    s_heads = [jnp.concatenate([_dot_nt(qqa_sc[h], r[head_rows[h], :].astype(BF16)) for r in dk_refs], axis=1)
               for h in range(A_HEADS)]
    z_pages = [jnp.concatenate([_dot(qqc_sc[pr], r[pair_rows[pr], :].astype(BF16)) for pr in range(n_pairs)], axis=0)
               for r in sk_refs]
    sp_pages = [_softplus2(z) for z in z_pages]
    later_pages = [_dot(sp.astype(BF16), tri) for sp in sp_pages]

    alphas, pbs = [], []
    for h in range(A_HEADS):
        m_prev = m_sc[h]
        m_new = jnp.maximum(m_prev, jnp.max(s_heads[h], axis=-1, keepdims=True))
        alpha = jnp.exp2(m_prev - m_new)
        p = jnp.exp2(s_heads[h] - _lane_tile(m_new, pps * page))
        m_sc[h] = m_new
        l_sc[h] = alpha * l_sc[h] + jnp.sum(p, axis=-1, keepdims=True)
        alphas.append(alpha)
        pbs.append(p.astype(BF16))
    carry = jnp.concatenate([carry_sc[pr] for pr in range(n_pairs)], axis=0)
    wbs = []
    for i in range(pps):
        wbs.append(jnp.exp2(z_pages[i] - sp_pages[i] + later_pages[i][:, :page] - carry).astype(BF16))
        carry = carry - later_pages[i][:, page:]
    for pr in range(n_pairs):
        carry_sc[pr] = carry[pr * 2 * t:(pr + 1) * 2 * t]

    for h in range(A_HEADS):
        pv = _dot(pbs[h][:, :page], dv_refs[0][head_rows[h], :].astype(BF16))
        for i in range(1, pps):
            pv = pv + _dot(pbs[h][:, i * page:(i + 1) * page], dv_refs[i][head_rows[h], :].astype(BF16))
        acca_sc[h] = alphas[h] * acca_sc[h] + pv
    for pr in range(n_pairs):
        rows = slice(pr * 2 * t, (pr + 1) * 2 * t)
        pv = _dot_nt(wbs[0][rows], sv_refs[0][pair_rows[pr], :].astype(BF16))
        for i in range(1, pps):
            pv = pv + _dot_nt(wbs[i][rows], sv_refs[i][pair_rows[pr], :].astype(BF16))
        accc_sc[pr] += pv

    @pl.when(j == pl.num_programs(1) - 1)
    def _finish():
        lam = _lambda(lq1_ref, lk1_ref, lq2_ref, lk2_ref, lam_init)
        lane = lax.broadcasted_iota(jnp.int32, (t, LANES), 1)
        for h in range(A_HEADS):
            cols = slice(h * LANES, (h + 1) * LANES)
            o = _diff_finish(acca_sc[h], l_sc[h], lam, sub_ref[...], ga_ref[:, cols].astype(F32), lam_init)
            oa_ref[:, cols] = o.astype(oa_ref.dtype)
        for pr in range(C_HEADS // 2):
            cols = slice(pr * LANES, (pr + 1) * LANES)
            acc = accc_sc[pr]
            o = jnp.where(lane < LANES // 2, acc[:t], acc[t:])
            oc_ref[:, cols] = (o * gc_ref[:, cols].astype(F32)).astype(oc_ref.dtype)


def _decode(layer, page_table, cache_dk, cache_dv, cache_sk, cache_sv,
            q_a, k_a_new, v_a_new, gate_a, q_c, k_c_new, v_c_new, gate_c, subln_g, lams,
            *, dec_batch, dec_seq, pages_per_step, lam_init):
    n_pages = page_table.shape[1]
    depth, n_pool, page = cache_dk.shape[:3]
    pps = pages_per_step
    assert n_pages % pps == 0
    n_steps = n_pages // pps
    cdk = cache_dk.reshape(depth, n_pool, page * A_HEADS, 2 * A_DHEAD)
    cdv = cache_dv.reshape(depth, n_pool, page * A_HEADS, 2 * A_DHEAD)
    csk = jnp.transpose(cache_sk, (0, 1, 3, 4, 2)).reshape(depth, n_pool, C_WIDTH, page)
    csv = jnp.transpose(cache_sv, (0, 1, 3, 4, 2)).reshape(depth, n_pool, C_WIDTH, page)
    pt_flat = page_table.reshape(-1)

    per_seq = lambda a: a.reshape(dec_batch, dec_seq, a.shape[-1])
    pad_page = lambda a: jnp.pad(per_seq(a), ((0, 0), (0, page - dec_seq), (0, 0)))
    seq_spec = lambda rows, width: pl.BlockSpec((None, rows, width), lambda b, j, pt: (b, 0, 0))
    const_spec = lambda width: pl.BlockSpec((1, width), lambda b, j, pt: (0, 0))

    def page_spec(i, rows, width):
        def index_map(b, j, pt):
            return (layer, pt[b * n_pages + (n_pages - 1 - (j * pps + i))], 0, 0)
        return pl.BlockSpec((None, None, rows, width), index_map)

    in_specs = ([seq_spec(dec_seq, A_WIDTH), seq_spec(page, A_WIDTH), seq_spec(page, A_WIDTH),
                 seq_spec(dec_seq, A_WIDTH),
                 seq_spec(dec_seq, C_WIDTH), seq_spec(page, C_WIDTH), seq_spec(page, C_WIDTH),
                 seq_spec(dec_seq, C_WIDTH),
                 const_spec(LANES)] + [const_spec(A_DHEAD)] * 4
                + [page_spec(i, page * A_HEADS, 2 * A_DHEAD) for i in range(pps)] * 2
                + [page_spec(i, C_WIDTH, page) for i in range(pps)] * 2)
    t2 = 2 * dec_seq
    stat_a = pltpu.VMEM((A_HEADS, t2, LANES), F32)
    stat_c = pltpu.VMEM((C_HEADS // 2, t2, LANES), F32)
    grid_spec = pltpu.PrefetchScalarGridSpec(
        num_scalar_prefetch=1,
        grid=(dec_batch, n_steps),
        in_specs=in_specs,
        out_specs=[seq_spec(dec_seq, A_WIDTH), seq_spec(dec_seq, C_WIDTH)],
        scratch_shapes=[pltpu.VMEM((A_HEADS, t2, LANES), BF16), pltpu.VMEM((C_HEADS // 2, t2, LANES), BF16),
                        stat_a, stat_a, stat_a, stat_c, stat_c])
    o_a, o_c = pl.pallas_call(
        functools.partial(_decode_kernel, pages_per_step=pps, lam_init=lam_init),
        grid_spec=grid_spec,
        out_shape=[jax.ShapeDtypeStruct((dec_batch, dec_seq, A_WIDTH), F32),
                   jax.ShapeDtypeStruct((dec_batch, dec_seq, C_WIDTH), F32)],
        compiler_params=pltpu.CompilerParams(dimension_semantics=("parallel", "arbitrary"),
                                             vmem_limit_bytes=VMEM_LIMIT),
        name="decode_attn",
    )(pt_flat, per_seq(q_a), pad_page(k_a_new), pad_page(v_a_new), per_seq(gate_a),
      per_seq(q_c), pad_page(k_c_new), pad_page(v_c_new), per_seq(gate_c),
      subln_g.reshape(1, LANES), *[a.reshape(1, A_DHEAD) for a in lams],
      *([cdk] * pps), *([cdv] * pps), *([csk] * pps), *([csv] * pps))
    n = dec_batch * dec_seq
    return o_a.reshape(n, A_WIDTH).astype(BF16), o_c.reshape(n, C_WIDTH).astype(BF16)


PROMPT_ROWS = 512
ATTN_BLOCK = 1024
ATTN_ROW_CHUNK = 256
SB_SUB = 256
PAGES_PER_STEP = 8


def kernel(x_prompt, x_sample, cache_diff_k, cache_diff_v, cache_sb_k, cache_sb_v, page_table, w_in, w_out, norm_pre_g, norm_post_g, lambda_q1, lambda_k1, lambda_q2, lambda_k2, diff_subln_g, sgu_ln_g, sgu_ln_b, sgu_w, sgu_b):
    batch, seq, d = x_prompt.shape
    dec_batch, dec_seq, _ = x_sample.shape
    depth = w_in.shape[0]
    w_in_b = w_in.astype(BF16)
    w_out_b = w_out.astype(BF16)
    hp = x_prompt.reshape(batch * seq, d)
    hs = x_sample.reshape(dec_batch * dec_seq, d)
    block = min(ATTN_BLOCK, seq)
    sub = min(SB_SUB, block)
    row_chunk = min(ATTN_ROW_CHUNK, block)
    prompt_rows = min(PROMPT_ROWS, seq)
    new_kv_prompt = None
    outs_s = [[] for _ in range(5)]
    for l in range(depth):
        lam_init = 0.8 - 0.6 * math.exp(-0.3 * l)
        lams = (lambda_q1[l], lambda_k1[l], lambda_q2[l], lambda_k2[l])

        (ak, av, ck, cv, _, qa, ka, va, ga, ob, qc, kc, vc, gc) = _in_proj(
            hp, norm_pre_g[l], w_in_b[l], sgu_ln_g[l], sgu_ln_b[l], sgu_w[l], sgu_b[l],
            seq_len=seq, block_rows=prompt_rows, stacked=(l, depth, new_kv_prompt))
        new_kv_prompt = (ak, av, ck, cv)
        oa = _diff_attn(qa, ka, va, ga, diff_subln_g[l], lams, batch=batch, seq_len=seq, block=block,
                        row_chunk=row_chunk, lam_init=lam_init)
        oc = _sb_attn(qc, kc, vc, gc, batch=batch, seq_len=seq, block=block, sub=sub, row_chunk=row_chunk)
        hp = _out_proj(oa, ob, oc, hp, w_out_b[l], norm_post_g[l], block_rows=prompt_rows)

        (ak, av, ck, cv, vn, qa, ka, va, ga, ob, qc, kc, vc, gc) = _in_proj(
            hs, norm_pre_g[l], w_in_b[l], sgu_ln_g[l], sgu_ln_b[l], sgu_w[l], sgu_b[l],
            seq_len=dec_seq, block_rows=dec_batch * dec_seq)
        oa, oc = _decode(l, page_table, cache_diff_k, cache_diff_v, cache_sb_k, cache_sb_v,
                         qa, ka, va, ga, qc, kc, vc, gc, diff_subln_g[l], lams,
                         dec_batch=dec_batch, dec_seq=dec_seq, pages_per_step=PAGES_PER_STEP,
                         lam_init=lam_init)
        hs = _out_proj(oa, ob, oc, hs, w_out_b[l], norm_post_g[l], block_rows=dec_batch * dec_seq)
        for dst, a, heads in zip(outs_s, (ak, av, ck, cv), (A_HEADS, A_HEADS, C_HEADS, C_HEADS)):
            dst.append(a.reshape(dec_batch, dec_seq, heads, a.shape[-1] // heads))
        outs_s[4].append(vn.reshape(dec_batch, dec_seq, B_WIDTH))

    ak, av, ck, cv = new_kv_prompt
    outs_p = [a.reshape(depth, batch, seq, A_HEADS, 2 * A_DHEAD) for a in (ak, av)]
    outs_p += [jnp.transpose(a.reshape(depth, batch, C_HEADS, C_DHEAD, seq), (0, 1, 4, 2, 3)) for a in (ck, cv)]
    stack = lambda xs: jnp.stack(xs, 0)
    return (hp.reshape(batch, seq, d), hs.reshape(dec_batch, dec_seq, d),
            *outs_p, *[stack(o) for o in outs_s])
```

```python
import functools
import math

import jax
import jax.numpy as jnp
from jax import lax
from jax.experimental import pallas as pl
from jax.experimental.pallas import tpu as pltpu

A_HEADS = 4
A_DHEAD = 64
A_WIDTH = A_HEADS * 2 * A_DHEAD
B_GROUPS = 4
B_DGROUP = 64
B_WIDTH = B_GROUPS * B_DGROUP
CHUNK = 128
C_HEADS = 4
C_DHEAD = 64
C_WIDTH = C_HEADS * C_DHEAD
IN_COLS = 4 * A_WIDTH + 3 * B_WIDTH + 4 * C_WIDTH
EPS = 1e-6
LANES = 128
NEG_BIG = -1e30
LOG2E = 1.4426950408889634
VMEM_LIMIT = 56 * 1024 * 1024

_OFF = {}
_o = 0
for _name, _w in (("a_q", A_WIDTH), ("a_k", A_WIDTH), ("a_v", A_WIDTH), ("a_g", A_WIDTH),
                  ("b_u", B_WIDTH), ("b_v", B_WIDTH), ("b_g", B_WIDTH),
                  ("c_q", C_WIDTH), ("c_k", C_WIDTH), ("c_v", C_WIDTH), ("c_g", C_WIDTH)):
    _OFF[_name] = (_o, _w)
    _o += _w

BF16 = jnp.bfloat16
F32 = jnp.float32


def _silu(x):
    return x / (1.0 + jnp.exp(-x))


def _softplus2(z):
    neg_abs = lax.bitcast_convert_type(lax.bitcast_convert_type(z, jnp.uint32) | jnp.uint32(0x80000000), F32)
    return jnp.maximum(z, 0.0) + jnp.log2(1.0 + jnp.exp2(neg_abs))


def _lane_tile(x, width):
    return x if width == LANES else jnp.concatenate([x] * (width // LANES), axis=1)


def _dot_nt(a, b):
    return lax.dot_general(a, b, (((1,), (1,)), ((), ())), preferred_element_type=F32)


def _dot(a, b):
    return jnp.dot(a, b, preferred_element_type=F32)


def _in_proj_kernel(x_ref, gpre_ref, w_ref, lng_ref, lnb_ref, wt_ref, bias_ref, *rest,
                    chunk_len, mm_rows, cache_layout, n_aliased, own_slab):
    if cache_layout:
        kv_refs = rest[n_aliased:4 + n_aliased]
        if own_slab is not None:
            for ref in kv_refs:
                for slab in range(ref.shape[0]):
                    if slab != own_slab:
                        ref[slab] = jnp.zeros(ref.shape[1:], ref.dtype)
            kv_refs = [ref.at[own_slab] for ref in kv_refs]
        ak_ref, av_ref, ck_ref, cv_ref = kv_refs
        vn_ref = None
        qa_ref, ka_ref, va_ref, ga_ref, ob_ref, qc_ref, kc_ref, vc_ref, gc_ref = rest[4 + n_aliased:]
    else:
        ak_ref, av_ref, ck_ref, cv_ref, vn_ref = rest[:5]
        qa_ref, ka_ref, va_ref, ga_ref, ob_ref, qc_ref, kc_ref, vc_ref, gc_ref = rest[5:]
    tm = x_ref.shape[0]
    x = x_ref[...]
    h = x * lax.rsqrt(jnp.mean(x * x, axis=-1, keepdims=True) + EPS) * gpre_ref[...]
    hb = h.astype(BF16)

    def proj(name):
        off, width = _OFF[name]
        return _dot(hb, w_ref[:, off:off + width])

    def store_a(ref, val):
        if cache_layout:
            for head in range(A_HEADS):
                ref[pl.ds(head, tm, stride=A_HEADS), :] = val[:, head * LANES:(head + 1) * LANES]
        else:
            ref[...] = val

    a_q = proj("a_q")
    qa_ref[...] = (a_q * (A_DHEAD ** -0.5 * LOG2E)).astype(BF16)
    a_k = proj("a_k")
    store_a(ak_ref, a_k)
    ka_ref[...] = a_k.astype(BF16)
    a_v = proj("a_v")
    store_a(av_ref, a_v)
    va_ref[...] = a_v.astype(BF16)
    ga_ref[...] = _silu(proj("a_g")).astype(BF16)
    c_q = proj("c_q")
    qc_ref[...] = (c_q * (C_DHEAD ** -0.5 * LOG2E)).astype(BF16)
    c_k = proj("c_k")
    kc_ref[...] = c_k.astype(BF16)
    c_v = proj("c_v")
    vc_ref[...] = c_v.astype(BF16)
    if cache_layout:
        ck_ref[...] = c_k.T
        cv_ref[...] = c_v.T
    else:
        ck_ref[...] = c_k
        cv_ref[...] = c_v
    gc_ref[...] = _silu(proj("c_g")).astype(BF16)

    b_v = proj("b_v")
    mu = jnp.mean(b_v, axis=-1, keepdims=True)
    cen = b_v - mu
    var = jnp.mean(cen * cen, axis=-1, keepdims=True)
    vn = cen * lax.rsqrt(var + EPS) * lng_ref[...] + lnb_ref[...]
    if vn_ref is not None:
        vn_ref[...] = vn
    vnb = vn.astype(BF16)

    r = lax.broadcasted_iota(jnp.int32, (B_GROUPS * mm_rows, mm_rows), 0) % mm_rows
    c = lax.broadcasted_iota(jnp.int32, (B_GROUPS * mm_rows, mm_rows), 1)
    keep = (r // chunk_len == c // chunk_len) & (c % chunk_len <= r % chunk_len)
    wt = jnp.where(keep, wt_ref[...], 0.0).astype(BF16)
    grp = lax.broadcasted_iota(jnp.int32, (mm_rows, B_WIDTH), 1) // B_DGROUP
    b_u = proj("b_u")
    gate_b = _silu(proj("b_g"))
    for i in range(tm // mm_rows):
        rows = slice(i * mm_rows, (i + 1) * mm_rows)
        full = _dot(wt, vnb[rows])
        mixed = bias_ref[...]
        for g in range(B_GROUPS):
            mixed = mixed + jnp.where(grp == g, full[g * mm_rows:(g + 1) * mm_rows], 0.0)
        ob_ref[rows, :] = (b_u[rows] * mixed * gate_b[rows]).astype(BF16)


def _in_proj(x, g_pre, w_in_bf16, ln_g, ln_b, sgu_w, sgu_b, *, seq_len, block_rows, stacked=None):
    n, d = x.shape
    chunk_len = min(seq_len, CHUNK)
    mm_rows = CHUNK if chunk_len == CHUNK else block_rows
    assert n % block_rows == 0 and block_rows % mm_rows == 0 and mm_rows % chunk_len == 0
    assert seq_len % chunk_len == 0
    rep = mm_rows // chunk_len
    wt = jnp.tile(sgu_w[:, :chunk_len, :chunk_len], (1, rep, rep)).reshape(B_GROUPS * mm_rows, mm_rows)
    bias = jnp.repeat(jnp.tile(sgu_b[:, :chunk_len], (1, rep)).T, B_DGROUP, axis=1)

    row = lambda width: pl.BlockSpec((block_rows, width), lambda i: (i, 0))
    const = lambda shape: pl.BlockSpec(shape, lambda i: (0, 0))
    operands = [x, g_pre.reshape(1, d), w_in_bf16, ln_g.reshape(1, B_WIDTH), ln_b.reshape(1, B_WIDTH), wt, bias]
    in_specs = [row(d), const((1, d)), const((d, IN_COLS)), const((1, B_WIDTH)), const((1, B_WIDTH)),
                const((B_GROUPS * mm_rows, mm_rows)), const((mm_rows, B_WIDTH))]
    bf16_shapes = [jax.ShapeDtypeStruct((n, wd), BF16)
                   for wd in (A_WIDTH, A_WIDTH, A_WIDTH, A_WIDTH, B_WIDTH, C_WIDTH, C_WIDTH, C_WIDTH, C_WIDTH)]
    aliases = {}
    own_slab = None
    if stacked is None:
        f32_shapes = [jax.ShapeDtypeStruct((n, wd), F32) for wd in (A_WIDTH, A_WIDTH, C_WIDTH, C_WIDTH, B_WIDTH)]
        f32_specs = [row(s.shape[1]) for s in f32_shapes]
        n_aliased = 0
    else:
        layer, depth, previous = stacked
        assert seq_len % block_rows == 0
        batch, per_seq = n // seq_len, seq_len // block_rows
        f32_shapes = ([jax.ShapeDtypeStruct((depth, n * A_HEADS, 2 * A_DHEAD), F32)] * 2
                      + [jax.ShapeDtypeStruct((depth, batch, C_WIDTH, seq_len), F32)] * 2)
        if previous is None:
            own_slab = layer
            f32_specs = ([pl.BlockSpec((depth, block_rows * A_HEADS, 2 * A_DHEAD), lambda i: (0, i, 0))] * 2
                         + [pl.BlockSpec((depth, None, C_WIDTH, block_rows),
                                         lambda i: (0, i // per_seq, 0, i % per_seq))] * 2)
        else:
            f32_specs = ([pl.BlockSpec((None, block_rows * A_HEADS, 2 * A_DHEAD), lambda i: (layer, i, 0))] * 2
                         + [pl.BlockSpec((None, None, C_WIDTH, block_rows),
                                         lambda i: (layer, i // per_seq, 0, i % per_seq))] * 2)
        n_aliased = 0 if previous is None else len(previous)
        for i in range(n_aliased):
            aliases[len(operands)] = i
            operands.append(previous[i])
            in_specs.append(pl.BlockSpec(memory_space=pl.ANY))
    out_shapes = f32_shapes + bf16_shapes
    outs = pl.pallas_call(
        functools.partial(_in_proj_kernel, chunk_len=chunk_len, mm_rows=mm_rows,
                          cache_layout=stacked is not None, n_aliased=n_aliased, own_slab=own_slab),
        grid=(n // block_rows,),
        in_specs=in_specs,
        out_specs=f32_specs + [row(s.shape[1]) for s in bf16_shapes],
        out_shape=out_shapes,
        input_output_aliases=aliases,
        compiler_params=pltpu.CompilerParams(dimension_semantics=("parallel",),
                                             vmem_limit_bytes=VMEM_LIMIT),
        name="in_proj",
    )(*operands)
    if stacked is not None:
        outs = list(outs[:4]) + [None] + list(outs[4:])
    return tuple(outs)


def _out_proj_kernel(oa_ref, ob_ref, oc_ref, x_ref, w_ref, g_ref, y_ref):
    mix = (_dot(oa_ref[...], w_ref[0:A_WIDTH, :])
           + _dot(ob_ref[...], w_ref[A_WIDTH:A_WIDTH + B_WIDTH, :])
           + _dot(oc_ref[...], w_ref[A_WIDTH + B_WIDTH:, :]))
    y_ref[...] = x_ref[...] + mix * lax.rsqrt(jnp.mean(mix * mix, axis=-1, keepdims=True) + EPS) * g_ref[...]


def _out_proj(o_a, o_b, o_c, x, w_out_bf16, g_post, *, block_rows):
    n, d = x.shape
    row = lambda width: pl.BlockSpec((block_rows, width), lambda i: (i, 0))
    return pl.pallas_call(
        _out_proj_kernel,
        grid=(n // block_rows,),
        in_specs=[row(A_WIDTH), row(B_WIDTH), row(C_WIDTH), row(d),
                  pl.BlockSpec(w_out_bf16.shape, lambda i: (0, 0)),
                  pl.BlockSpec((1, d), lambda i: (0, 0))],
        out_specs=row(d),
        out_shape=jax.ShapeDtypeStruct((n, d), F32),
        compiler_params=pltpu.CompilerParams(dimension_semantics=("parallel",),
                                             vmem_limit_bytes=VMEM_LIMIT),
        name="out_proj",
    )(o_a, o_b, o_c, x, w_out_bf16, g_post.reshape(1, d))


def _split_halves(q):
    lane = lax.broadcasted_iota(jnp.int32, q.shape, 1)
    zero = jnp.zeros_like(q)
    return jnp.concatenate([jnp.where(lane < LANES // 2, q, zero), jnp.where(lane >= LANES // 2, q, zero)], axis=0)


def _lambda(lq1_ref, lk1_ref, lq2_ref, lk2_ref, lam_init):
    s1 = jnp.sum(lq1_ref[...] * lk1_ref[...], axis=-1, keepdims=True)
    s2 = jnp.sum(lq2_ref[...] * lk2_ref[...], axis=-1, keepdims=True)
    return jnp.exp(s1) - jnp.exp(s2) + lam_init


def _diff_finish(acc, l, lam, subln_g, gate, lam_init):
    r = acc.shape[0] // 2
    o = acc[:r] / l[:r] - lam * (acc[r:] / l[r:])
    o = o * lax.rsqrt(jnp.mean(o * o, axis=-1, keepdims=True) + EPS) * subln_g * (1.0 - lam_init)
    return o * gate


def _softmax_update(s, m_prev, l_prev, acc_prev, v_blocks):
    m_new = jnp.maximum(m_prev, jnp.max(s, axis=-1, keepdims=True))
    alpha = jnp.exp2(m_prev - m_new)
    p = jnp.exp2(s - _lane_tile(m_new, s.shape[1]))
    pb = p.astype(BF16)
    pv, start = None, 0
    for v in v_blocks:
        part = _dot(pb[:, start:start + v.shape[0]], v)
        pv = part if pv is None else pv + part
        start += v.shape[0]
    if pv.shape[1] == 2 * LANES:
        row_sum, pv = pv[:, LANES:], pv[:, :LANES]
    else:
        row_sum = jnp.sum(p, axis=-1, keepdims=True)
    return m_new, alpha * l_prev + row_sum, alpha * acc_prev + pv


def _with_ones(v):
    return jnp.concatenate([v, jnp.ones(v.shape, v.dtype)], axis=1)


def _triangle_tables(n_blocks, descending):
    qs, ks = [], []
    for qi in range(n_blocks):
        order = range(qi, -1, -1) if descending else range(qi + 1)
        for ki in order:
            qs.append(qi)
            ks.append(ki)
    return jnp.asarray(qs, jnp.int32), jnp.asarray(ks, jnp.int32)


def _diff_attn_kernel(qtab_ref, ktab_ref, q_ref, k_ref, v_ref, gate_ref, sub_ref,
                      lq1_ref, lk1_ref, lq2_ref, lk2_ref, o_ref,
                      qq_sc, m_sc, l_sc, acc_sc, *, lam_init, row_chunk):
    s_idx = pl.program_id(2)
    qi = qtab_ref[s_idx]
    ki = ktab_ref[s_idx]
    tq = q_ref.shape[0]
    tk = k_ref.shape[0]

    @pl.when(ki == 0)
    def _init():
        qq_sc[...] = _split_halves(q_ref[...])
        m_sc[...] = jnp.full(m_sc.shape, NEG_BIG, F32)
        l_sc[...] = jnp.zeros(l_sc.shape, F32)
        acc_sc[...] = jnp.zeros(acc_sc.shape, F32)

    def step(masked):
        k = k_ref[...]
        v_ones = _with_ones(v_ref[...])
        chunks = []
        for c in range(2 * tq // row_chunk):
            rows = slice(c * row_chunk, (c + 1) * row_chunk)
            q0 = (c * row_chunk) % tq
            width = min(tk, q0 + row_chunk) if masked else tk
            s = _dot_nt(qq_sc[rows], k[:width])
            if masked:
                qpos = q0 + lax.broadcasted_iota(jnp.int32, s.shape, 0)
                kpos = lax.broadcasted_iota(jnp.int32, s.shape, 1)
                s = jnp.where(kpos <= qpos, s, NEG_BIG)
            chunks.append((rows, width, s))
        weights = []
        for rows, width, s in chunks:
            m_prev = m_sc[rows]
            m_new = jnp.maximum(m_prev, jnp.max(s, axis=-1, keepdims=True))
            m_sc[rows] = m_new
            weights.append((jnp.exp2(m_prev - m_new), jnp.exp2(s - _lane_tile(m_new, width)).astype(BF16)))
        for (rows, width, _), (alpha, pb) in zip(chunks, weights):
            pv = _dot(pb, v_ones[:width])
            l_sc[rows] = alpha * l_sc[rows] + pv[:, LANES:]
            acc_sc[rows] = alpha * acc_sc[rows] + pv[:, :LANES]

    @pl.when(ki < qi)
    def _off_diag():
        step(False)

    @pl.when(ki == qi)
    def _diag():
        step(True)
        lam = _lambda(lq1_ref, lk1_ref, lq2_ref, lk2_ref, lam_init)
        o = _diff_finish(acc_sc[...], l_sc[...], lam, sub_ref[...], gate_ref[...].astype(F32), lam_init)
        o_ref[...] = o.astype(o_ref.dtype)


def _diff_attn(q, k, v, gate, subln_g, lams, *, batch, seq_len, block, row_chunk, lam_init):
    n = q.shape[0]
    nb = seq_len // block
    qtab, ktab = _triangle_tables(nb, descending=False)
    qmap = lambda b, h, s, qt, kt: (b * nb + qt[s], h)
    kmap = lambda b, h, s, qt, kt: (b * nb + kt[s], h)
    cmap = lambda b, h, s, qt, kt: (0, 0)
    blk = lambda m: pl.BlockSpec((block, LANES), m)
    lam_spec = pl.BlockSpec((1, A_DHEAD), cmap)
    stat = pltpu.VMEM((2 * block, LANES), F32)
    grid_spec = pltpu.PrefetchScalarGridSpec(
        num_scalar_prefetch=2,
        grid=(batch, A_HEADS, int(qtab.shape[0])),
        in_specs=[blk(qmap), blk(kmap), blk(kmap), blk(qmap), pl.BlockSpec((1, LANES), cmap),
                  lam_spec, lam_spec, lam_spec, lam_spec],
        out_specs=blk(qmap),
        scratch_shapes=[pltpu.VMEM((2 * block, LANES), BF16), stat, stat, stat])
    return pl.pallas_call(
        functools.partial(_diff_attn_kernel, lam_init=lam_init, row_chunk=row_chunk),
        grid_spec=grid_spec,
        out_shape=jax.ShapeDtypeStruct((n, A_WIDTH), BF16),
        compiler_params=pltpu.CompilerParams(dimension_semantics=("parallel", "parallel", "arbitrary"),
                                             vmem_limit_bytes=VMEM_LIMIT),
        name="diff_attn",
    )(qtab, ktab, q, k, v, gate, subln_g.reshape(1, LANES), *[a.reshape(1, A_DHEAD) for a in lams])


def _neg_later(n, with_total):
    j = lax.broadcasted_iota(jnp.int32, (n, n), 0)
    k = lax.broadcasted_iota(jnp.int32, (n, n), 1)
    m = jnp.where(j > k, -1.0, 0.0).astype(BF16)
    return jnp.concatenate([m, jnp.full((n, LANES), -1.0, BF16)], axis=1) if with_total else m


def _sb_attn_kernel(qtab_ref, ktab_ref, q_ref, k_ref, v_ref, gate_ref, o_ref,
                    qq_sc, carry_sc, acc_sc, *, sub, row_chunk):
    s_idx = pl.program_id(2)
    qi = qtab_ref[s_idx]
    ki = ktab_ref[s_idx]
    tq = q_ref.shape[0]
    tk = k_ref.shape[0]

    @pl.when(ki == qi)
    def _init():
        qq_sc[...] = _split_halves(q_ref[...])
        carry_sc[...] = jnp.zeros(carry_sc.shape, F32)
        acc_sc[...] = jnp.zeros(acc_sc.shape, F32)

    def step(masked):
        k = k_ref[...]
        v = v_ref[...]
        tri = _neg_later(sub, with_total=False)
        chunks = []
        for c in range(2 * tq // row_chunk):
            rows = slice(c * row_chunk, (c + 1) * row_chunk)
            q0 = (c * row_chunk) % tq
            width = min(tk, -(-(q0 + row_chunk) // sub) * sub) if masked else tk
            z = _dot_nt(qq_sc[rows], k[:width])
            valid = None
            if masked:
                qpos = q0 + lax.broadcasted_iota(jnp.int32, z.shape, 0)
                kpos = lax.broadcasted_iota(jnp.int32, z.shape, 1)
                valid = kpos < qpos
            chunks.append((rows, width, z, valid))
        prefix = []
        for rows, width, z, valid in chunks:
            sp = _softplus2(z)
            if valid is not None:
                sp = jnp.where(valid, sp, 0.0)
            later = [_dot(sp[:, j * sub:(j + 1) * sub].astype(BF16), tri) for j in range(width // sub)]
            prefix.append((sp, later))
        weights = []
        for (rows, width, z, valid), (sp, later) in zip(chunks, prefix):
            carry = carry_sc[rows]
            ws = [None] * (width // sub)
            for j in range(width // sub - 1, -1, -1):
                cols = slice(j * sub, (j + 1) * sub)
                t = z[:, cols] - sp[:, cols] + later[j] - _lane_tile(carry, sub)
                if valid is not None:
                    t = jnp.where(valid[:, cols], t, NEG_BIG)
                ws[j] = jnp.exp2(t)
                carry = carry + jnp.sum(sp[:, cols], axis=-1, keepdims=True)
            carry_sc[rows] = carry
            weights.append(jnp.concatenate(ws, axis=1).astype(BF16))
        for (rows, width, _, _), wb in zip(chunks, weights):
            acc_sc[rows] += _dot(wb, v[:width])

    @pl.when(ki == qi)
    def _diag():
        step(True)

    @pl.when(ki < qi)
    def _off_diag():
        step(False)

    @pl.when(ki == 0)
    def _finish():
        acc = acc_sc[...]
        lane = lax.broadcasted_iota(jnp.int32, (tq, LANES), 1)
        o = jnp.where(lane < LANES // 2, acc[:tq], acc[tq:])
        o_ref[...] = (o * gate_ref[...].astype(F32)).astype(o_ref.dtype)


def _sb_attn(q, k, v, gate, *, batch, seq_len, block, sub, row_chunk):
    n = q.shape[0]
    nb = seq_len // block
    qtab, ktab = _triangle_tables(nb, descending=True)
    qmap = lambda b, h, s, qt, kt: (b * nb + qt[s], h)
    kmap = lambda b, h, s, qt, kt: (b * nb + kt[s], h)
    blk = lambda m: pl.BlockSpec((block, LANES), m)
    grid_spec = pltpu.PrefetchScalarGridSpec(
        num_scalar_prefetch=2,
        grid=(batch, C_HEADS // 2, int(qtab.shape[0])),
        in_specs=[blk(qmap), blk(kmap), blk(kmap), blk(qmap)],
        out_specs=blk(qmap),
        scratch_shapes=[pltpu.VMEM((2 * block, LANES), BF16), pltpu.VMEM((2 * block, LANES), F32),
                        pltpu.VMEM((2 * block, LANES), F32)])
    return pl.pallas_call(
        functools.partial(_sb_attn_kernel, sub=sub, row_chunk=row_chunk),
        grid_spec=grid_spec,
        out_shape=jax.ShapeDtypeStruct((n, C_WIDTH), BF16),
        compiler_params=pltpu.CompilerParams(dimension_semantics=("parallel", "parallel", "arbitrary"),
                                             vmem_limit_bytes=VMEM_LIMIT),
        name="sb_attn",
    )(qtab, ktab, q, k, v, gate)


def _decode_kernel(pt_ref, qa_ref, kan_ref, van_ref, ga_ref, qc_ref, kcn_ref, vcn_ref, gc_ref, sub_ref,
                   lq1_ref, lk1_ref, lq2_ref, lk2_ref, cdk_ref, cdv_ref, csk_ref, csv_ref, oa_ref, oc_ref,
                   qqa_sc, qqc_sc, m_sc, l_sc, acca_sc, carry_sc, accc_sc,
                   dk_a, dv_a, sk_a, sv_a, dk_b, dv_b, sk_b, sv_b, sems,
                   *, layer, n_pages, pages_per_half, lam_init):
    pph = pages_per_half
    b = pl.program_id(0)
    j = pl.program_id(1)
    n_j = pl.num_programs(1)
    step = b * n_j + j
    last_step = pl.num_programs(0) * n_j - 1
    t = qa_ref.shape[0]
    page = sk_a.shape[2]
    n_pairs = C_HEADS // 2
    tri = _neg_later(page, with_total=True)
    caches = (cdk_ref, cdv_ref, csk_ref, csv_ref)
    buffers = ((dk_a, dv_a, sk_a, sv_a), (dk_b, dv_b, sk_b, sv_b))

    def page_copy(slot, kind, i, page_id):
        return pltpu.make_async_copy(caches[kind].at[layer, page_id], buffers[slot][kind].at[i], sems.at[slot, kind])

    def start_page(slot, seq, first_page, i):
        page_id = pt_ref[seq * n_pages + (n_pages - 1 - (first_page + i))]
        for kind in range(4):
            page_copy(slot, kind, i, page_id).start()

    def wait_slot(slot):
        for i in range(pph):
            for kind in range(4):
                page_copy(slot, kind, i, 0).wait()

    def sb_page(z, valid, carry):
        sp = _softplus2(z)
        if valid is not None:
            sp = jnp.where(valid, sp, 0.0)
        later = _dot(sp.astype(BF16), tri)
        tt = z - sp + later[:, :page] - carry
        if valid is not None:
            tt = jnp.where(valid, tt, NEG_BIG)
        return jnp.exp2(tt), carry - later[:, page:]

    @pl.when(step == 0)
    def _prologue():
        for i in range(pph):
            start_page(0, 0, 0, i)

    @pl.when(j == 0)
    def _first():
        qpos = lax.broadcasted_iota(jnp.int32, (2 * t, page), 0) % t
        kpos = lax.broadcasted_iota(jnp.int32, (2 * t, page), 1)
        kan = kan_ref[...]
        van = van_ref[...]
        for h in range(A_HEADS):
            cols = slice(h * LANES, (h + 1) * LANES)
            qq = _split_halves(qa_ref[:, cols])
            qqa_sc[h] = qq
            s = jnp.where(kpos <= qpos, _dot_nt(qq, kan[:, cols]), NEG_BIG)
            m_sc[h], l_sc[h], acca_sc[h] = _softmax_update(
                s, jnp.full((2 * t, LANES), NEG_BIG, F32), jnp.zeros((2 * t, LANES), F32),
                jnp.zeros((2 * t, LANES), F32), [_with_ones(van[:, cols])])
        kcn = kcn_ref[...]
        vcn = vcn_ref[...]
        for pr in range(n_pairs):
            cols = slice(pr * LANES, (pr + 1) * LANES)
            qq = _split_halves(qc_ref[:, cols])
            qqc_sc[pr] = qq
            w, carry_sc[pr] = sb_page(_dot_nt(qq, kcn[:, cols]), kpos < qpos, jnp.zeros((2 * t, LANES), F32))
            accc_sc[pr] = _dot(w.astype(BF16), vcn[:, cols])

    head_rows = [pl.ds(h, page, stride=A_HEADS) for h in range(A_HEADS)]
    pair_rows = [slice(pr * LANES, (pr + 1) * LANES) for pr in range(n_pairs)]

    def half_step(slot, next_seq, next_first_page):
        dk, dv, sk, sv = buffers[slot]
        wait_slot(slot)
        started = [0]

        def start_some(count):
            for _ in range(count):
                start_page(1 - slot, next_seq, next_first_page, started[0])
                started[0] += 1

        per_head = pph // A_HEADS
        s_heads = []
        for h in range(A_HEADS):
            s_heads.append(jnp.concatenate(
                [_dot_nt(qqa_sc[h], dk[i, head_rows[h], :].astype(BF16)) for i in range(pph)], axis=1))
            start_some(per_head)
        start_some(pph - A_HEADS * per_head)
        z_pages = [jnp.concatenate([_dot(qqc_sc[pr], sk[i, pair_rows[pr], :].astype(BF16)) for pr in range(n_pairs)],
                                   axis=0) for i in range(pph)]
        sp_pages = [_softplus2(z) for z in z_pages]
        later_pages = [_dot(sp.astype(BF16), tri) for sp in sp_pages]

        alphas, pbs = [], []
        for h in range(A_HEADS):
            m_prev = m_sc[h]
            m_new = jnp.maximum(m_prev, jnp.max(s_heads[h], axis=-1, keepdims=True))
            alpha = jnp.exp2(m_prev - m_new)
            p = jnp.exp2(s_heads[h] - _lane_tile(m_new, pph * page))
            m_sc[h] = m_new
            l_sc[h] = alpha * l_sc[h] + jnp.sum(p, axis=-1, keepdims=True)
            alphas.append(alpha)
            pbs.append(p.astype(BF16))
        carry = jnp.concatenate([carry_sc[pr] for pr in range(n_pairs)], axis=0)
        wbs = []
        for i in range(pph):
            wbs.append(jnp.exp2(z_pages[i] - sp_pages[i] + later_pages[i][:, :page] - carry).astype(BF16))
            carry = carry - later_pages[i][:, page:]
        for pr in range(n_pairs):
            carry_sc[pr] = carry[pr * 2 * t:(pr + 1) * 2 * t]

        for h in range(A_HEADS):
            pv = _dot(pbs[h][:, :page], dv[0, head_rows[h], :].astype(BF16))
            for i in range(1, pph):
                pv = pv + _dot(pbs[h][:, i * page:(i + 1) * page], dv[i, head_rows[h], :].astype(BF16))
            acca_sc[h] = alphas[h] * acca_sc[h] + pv
        for pr in range(n_pairs):
            rows = slice(pr * 2 * t, (pr + 1) * 2 * t)
            pv = _dot_nt(wbs[0][rows], sv[0, pair_rows[pr], :].astype(BF16))
            for i in range(1, pph):
                pv = pv + _dot_nt(wbs[i][rows], sv[i, pair_rows[pr], :].astype(BF16))
            accc_sc[pr] += pv

    half_step(0, b, (2 * j + 1) * pph)
    nxt = jnp.minimum(step + 1, last_step)
    half_step(1, nxt // n_j, (nxt % n_j) * 2 * pph)

    @pl.when(j == n_j - 1)
    def _finish():
        lam = _lambda(lq1_ref, lk1_ref, lq2_ref, lk2_ref, lam_init)
        lane = lax.broadcasted_iota(jnp.int32, (t, LANES), 1)
        for h in range(A_HEADS):
            cols = slice(h * LANES, (h + 1) * LANES)
            o = _diff_finish(acca_sc[h], l_sc[h], lam, sub_ref[...], ga_ref[:, cols].astype(F32), lam_init)
            oa_ref[:, cols] = o.astype(oa_ref.dtype)
        for pr in range(n_pairs):
            cols = slice(pr * LANES, (pr + 1) * LANES)
            acc = accc_sc[pr]
            o = jnp.where(lane < LANES // 2, acc[:t], acc[t:])
            oc_ref[:, cols] = (o * gc_ref[:, cols].astype(F32)).astype(oc_ref.dtype)

    @pl.when(step == last_step)
    def _drain():
        wait_slot(0)


def _decode(layer, page_table, cache_dk, cache_dv, cache_sk, cache_sv,
            q_a, k_a_new, v_a_new, gate_a, q_c, k_c_new, v_c_new, gate_c, subln_g, lams,
            *, dec_batch, dec_seq, pages_per_half, lam_init):
    n_pages = page_table.shape[1]
    depth, n_pool, page = cache_dk.shape[:3]
    pph = pages_per_half
    assert n_pages % (2 * pph) == 0
    n_steps = n_pages // (2 * pph)
    cdk = cache_dk.reshape(depth, n_pool, page * A_HEADS, 2 * A_DHEAD)
    cdv = cache_dv.reshape(depth, n_pool, page * A_HEADS, 2 * A_DHEAD)
    csk = jnp.transpose(cache_sk, (0, 1, 3, 4, 2)).reshape(depth, n_pool, C_WIDTH, page)
    csv = jnp.transpose(cache_sv, (0, 1, 3, 4, 2)).reshape(depth, n_pool, C_WIDTH, page)
    pt_flat = page_table.reshape(-1)

    per_seq = lambda a: a.reshape(dec_batch, dec_seq, a.shape[-1])
    pad_page = lambda a: jnp.pad(per_seq(a), ((0, 0), (0, page - dec_seq), (0, 0)))
    seq_spec = lambda rows, width: pl.BlockSpec((None, rows, width), lambda b, j, pt: (b, 0, 0))
    const_spec = lambda width: pl.BlockSpec((1, width), lambda b, j, pt: (0, 0))
    hbm_spec = pl.BlockSpec(memory_space=pl.ANY)

    in_specs = ([seq_spec(dec_seq, A_WIDTH), seq_spec(page, A_WIDTH), seq_spec(page, A_WIDTH),
                 seq_spec(dec_seq, A_WIDTH),
                 seq_spec(dec_seq, C_WIDTH), seq_spec(page, C_WIDTH), seq_spec(page, C_WIDTH),
                 seq_spec(dec_seq, C_WIDTH),
                 const_spec(LANES)] + [const_spec(A_DHEAD)] * 4 + [hbm_spec] * 4)
    t2 = 2 * dec_seq
    stat_a = pltpu.VMEM((A_HEADS, t2, LANES), F32)
    stat_c = pltpu.VMEM((C_HEADS // 2, t2, LANES), F32)
    page_bufs = [pltpu.VMEM((pph, page * A_HEADS, 2 * A_DHEAD), F32)] * 2 + [pltpu.VMEM((pph, C_WIDTH, page), F32)] * 2
    grid_spec = pltpu.PrefetchScalarGridSpec(
        num_scalar_prefetch=1,
        grid=(dec_batch, n_steps),
        in_specs=in_specs,
        out_specs=[seq_spec(dec_seq, A_WIDTH), seq_spec(dec_seq, C_WIDTH)],
        scratch_shapes=[pltpu.VMEM((A_HEADS, t2, LANES), BF16), pltpu.VMEM((C_HEADS // 2, t2, LANES), BF16),
                        stat_a, stat_a, stat_a, stat_c, stat_c] + page_bufs * 2
                       + [pltpu.SemaphoreType.DMA((2, 4))])
    o_a, o_c = pl.pallas_call(
        functools.partial(_decode_kernel, layer=layer, n_pages=n_pages, pages_per_half=pph, lam_init=lam_init),
        grid_spec=grid_spec,
        out_shape=[jax.ShapeDtypeStruct((dec_batch, dec_seq, A_WIDTH), F32),
                   jax.ShapeDtypeStruct((dec_batch, dec_seq, C_WIDTH), F32)],
        compiler_params=pltpu.CompilerParams(dimension_semantics=("arbitrary", "arbitrary"),
                                             vmem_limit_bytes=VMEM_LIMIT),
        name="decode_attn",
    )(pt_flat, per_seq(q_a), pad_page(k_a_new), pad_page(v_a_new), per_seq(gate_a),
      per_seq(q_c), pad_page(k_c_new), pad_page(v_c_new), per_seq(gate_c),
      subln_g.reshape(1, LANES), *[a.reshape(1, A_DHEAD) for a in lams], cdk, cdv, csk, csv)
    n = dec_batch * dec_seq
    return o_a.reshape(n, A_WIDTH).astype(BF16), o_c.reshape(n, C_WIDTH).astype(BF16)


PROMPT_ROWS = 512
ATTN_BLOCK = 1024
ATTN_ROW_CHUNK = 256
SB_SUB = 256
PAGES_PER_HALF = 8


def kernel(x_prompt, x_sample, cache_diff_k, cache_diff_v, cache_sb_k, cache_sb_v, page_table, w_in, w_out, norm_pre_g, norm_post_g, lambda_q1, lambda_k1, lambda_q2, lambda_k2, diff_subln_g, sgu_ln_g, sgu_ln_b, sgu_w, sgu_b):
    batch, seq, d = x_prompt.shape
    dec_batch, dec_seq, _ = x_sample.shape
    depth = w_in.shape[0]
    w_in_b = w_in.astype(BF16)
    w_out_b = w_out.astype(BF16)
    hp = x_prompt.reshape(batch * seq, d)
    hs = x_sample.reshape(dec_batch * dec_seq, d)
    block = min(ATTN_BLOCK, seq)
    sub = min(SB_SUB, block)
    row_chunk = min(ATTN_ROW_CHUNK, block)
    prompt_rows = min(PROMPT_ROWS, seq)
    new_kv_prompt = None
    outs_s = [[] for _ in range(5)]
    for l in range(depth):
        lam_init = 0.8 - 0.6 * math.exp(-0.3 * l)
        lams = (lambda_q1[l], lambda_k1[l], lambda_q2[l], lambda_k2[l])

        (ak, av, ck, cv, _, qa, ka, va, ga, ob, qc, kc, vc, gc) = _in_proj(
            hp, norm_pre_g[l], w_in_b[l], sgu_ln_g[l], sgu_ln_b[l], sgu_w[l], sgu_b[l],
            seq_len=seq, block_rows=prompt_rows, stacked=(l, depth, new_kv_prompt))
        new_kv_prompt = (ak, av, ck, cv)
        oa = _diff_attn(qa, ka, va, ga, diff_subln_g[l], lams, batch=batch, seq_len=seq, block=block,
                        row_chunk=row_chunk, lam_init=lam_init)
        oc = _sb_attn(qc, kc, vc, gc, batch=batch, seq_len=seq, block=block, sub=sub, row_chunk=row_chunk)
        hp = _out_proj(oa, ob, oc, hp, w_out_b[l], norm_post_g[l], block_rows=prompt_rows)

        (ak, av, ck, cv, vn, qa, ka, va, ga, ob, qc, kc, vc, gc) = _in_proj(
            hs, norm_pre_g[l], w_in_b[l], sgu_ln_g[l], sgu_ln_b[l], sgu_w[l], sgu_b[l],
            seq_len=dec_seq, block_rows=dec_batch * dec_seq)
        oa, oc = _decode(l, page_table, cache_diff_k, cache_diff_v, cache_sb_k, cache_sb_v,
                         qa, ka, va, ga, qc, kc, vc, gc, diff_subln_g[l], lams,
                         dec_batch=dec_batch, dec_seq=dec_seq, pages_per_half=PAGES_PER_HALF,
                         lam_init=lam_init)
        hs = _out_proj(oa, ob, oc, hs, w_out_b[l], norm_post_g[l], block_rows=dec_batch * dec_seq)
        for dst, a, heads in zip(outs_s, (ak, av, ck, cv), (A_HEADS, A_HEADS, C_HEADS, C_HEADS)):
            dst.append(a.reshape(dec_batch, dec_seq, heads, a.shape[-1] // heads))
        outs_s[4].append(vn.reshape(dec_batch, dec_seq, B_WIDTH))

    ak, av, ck, cv = new_kv_prompt
    outs_p = [a.reshape(depth, batch, seq, A_HEADS, 2 * A_DHEAD) for a in (ak, av)]
    outs_p += [jnp.transpose(a.reshape(depth, batch, C_HEADS, C_DHEAD, seq), (0, 1, 4, 2, 3)) for a in (ck, cv)]
    stack = lambda xs: jnp.stack(xs, 0)
    return (hp.reshape(batch, seq, d), hs.reshape(dec_batch, dec_seq, d),
            *outs_p, *[stack(o) for o in outs_s])
```

```python
import functools
import math

import jax
import jax.numpy as jnp
from jax import lax
from jax.experimental import pallas as pl
from jax.experimental.pallas import tpu as pltpu

A_HEADS = 4
A_DHEAD = 64
A_WIDTH = A_HEADS * 2 * A_DHEAD
B_GROUPS = 4
B_DGROUP = 64
B_WIDTH = B_GROUPS * B_DGROUP
CHUNK = 128
C_HEADS = 4
C_DHEAD = 64
C_WIDTH = C_HEADS * C_DHEAD
IN_COLS = 4 * A_WIDTH + 3 * B_WIDTH + 4 * C_WIDTH
EPS = 1e-6
LANES = 128
NEG_BIG = -1e30
LOG2E = 1.4426950408889634
VMEM_LIMIT = 56 * 1024 * 1024

_OFF = {}
_o = 0
for _name, _w in (("a_q", A_WIDTH), ("a_k", A_WIDTH), ("a_v", A_WIDTH), ("a_g", A_WIDTH),
                  ("b_u", B_WIDTH), ("b_v", B_WIDTH), ("b_g", B_WIDTH),
                  ("c_q", C_WIDTH), ("c_k", C_WIDTH), ("c_v", C_WIDTH), ("c_g", C_WIDTH)):
    _OFF[_name] = (_o, _w)
    _o += _w

BF16 = jnp.bfloat16
F32 = jnp.float32


def _silu(x):
    return x / (1.0 + jnp.exp(-x))


def _softplus2(z):
    neg_abs = lax.bitcast_convert_type(lax.bitcast_convert_type(z, jnp.uint32) | jnp.uint32(0x80000000), F32)
    return jnp.maximum(z, 0.0) + jnp.log2(1.0 + jnp.exp2(neg_abs))


def _lane_tile(x, width):
    return x if width == LANES else jnp.concatenate([x] * (width // LANES), axis=1)


def _dot_nt(a, b):
    return lax.dot_general(a, b, (((1,), (1,)), ((), ())), preferred_element_type=F32)


def _dot(a, b):
    return jnp.dot(a, b, preferred_element_type=F32)


def _in_proj_kernel(x_ref, gpre_ref, w_ref, lng_ref, lnb_ref, wt_ref, bias_ref, *rest,
                    chunk_len, mm_rows, cache_layout, n_aliased, own_slab):
    if cache_layout:
        kv_refs = rest[n_aliased:4 + n_aliased]
        if own_slab is not None:
            for ref in kv_refs:
                for slab in range(ref.shape[0]):
                    if slab != own_slab:
                        ref[slab] = jnp.zeros(ref.shape[1:], ref.dtype)
            kv_refs = [ref.at[own_slab] for ref in kv_refs]
        ak_ref, av_ref, ck_ref, cv_ref = kv_refs
        vn_ref = None
        qa_ref, ka_ref, va_ref, ga_ref, ob_ref, qc_ref, kc_ref, vc_ref, gc_ref = rest[4 + n_aliased:]
    else:
        ak_ref, av_ref, ck_ref, cv_ref, vn_ref = rest[:5]
        qa_ref, ka_ref, va_ref, ga_ref, ob_ref, qc_ref, kc_ref, vc_ref, gc_ref = rest[5:]
    tm = x_ref.shape[0]
    x = x_ref[...]
    h = x * lax.rsqrt(jnp.mean(x * x, axis=-1, keepdims=True) + EPS) * gpre_ref[...]
    hb = h.astype(BF16)

    def proj(name):
        off, width = _OFF[name]
        return _dot(hb, w_ref[:, off:off + width])

    def store_a(ref, val):
        if cache_layout:
            for head in range(A_HEADS):
                ref[pl.ds(head, tm, stride=A_HEADS), :] = val[:, head * LANES:(head + 1) * LANES]
        else:
            ref[...] = val

    a_q = proj("a_q")
    qa_ref[...] = (a_q * (A_DHEAD ** -0.5 * LOG2E)).astype(BF16)
    a_k = proj("a_k")
    store_a(ak_ref, a_k)
    ka_ref[...] = a_k.astype(BF16)
    a_v = proj("a_v")
    store_a(av_ref, a_v)
    va_ref[...] = a_v.astype(BF16)
    ga_ref[...] = _silu(proj("a_g")).astype(BF16)
    c_q = proj("c_q")
    qc_ref[...] = (c_q * (C_DHEAD ** -0.5 * LOG2E)).astype(BF16)
    c_k = proj("c_k")
    kc_ref[...] = c_k.astype(BF16)
    c_v = proj("c_v")
    vc_ref[...] = c_v.astype(BF16)
    if cache_layout:
        ck_ref[...] = c_k.T
        cv_ref[...] = c_v.T
    else:
        ck_ref[...] = c_k
        cv_ref[...] = c_v
    gc_ref[...] = _silu(proj("c_g")).astype(BF16)

    b_v = proj("b_v")
    mu = jnp.mean(b_v, axis=-1, keepdims=True)
    cen = b_v - mu
    var = jnp.mean(cen * cen, axis=-1, keepdims=True)
    vn = cen * lax.rsqrt(var + EPS) * lng_ref[...] + lnb_ref[...]
    if vn_ref is not None:
        vn_ref[...] = vn
    vnb = vn.astype(BF16)

    r = lax.broadcasted_iota(jnp.int32, (B_GROUPS * mm_rows, mm_rows), 0) % mm_rows
    c = lax.broadcasted_iota(jnp.int32, (B_GROUPS * mm_rows, mm_rows), 1)
    keep = (r // chunk_len == c // chunk_len) & (c % chunk_len <= r % chunk_len)
    wt = jnp.where(keep, wt_ref[...], 0.0).astype(BF16)
    grp = lax.broadcasted_iota(jnp.int32, (mm_rows, B_WIDTH), 1) // B_DGROUP
    b_u = proj("b_u")
    gate_b = _silu(proj("b_g"))
    for i in range(tm // mm_rows):
        rows = slice(i * mm_rows, (i + 1) * mm_rows)
        full = _dot(wt, vnb[rows])
        mixed = bias_ref[...]
        for g in range(B_GROUPS):
            mixed = mixed + jnp.where(grp == g, full[g * mm_rows:(g + 1) * mm_rows], 0.0)
        ob_ref[rows, :] = (b_u[rows] * mixed * gate_b[rows]).astype(BF16)


def _in_proj(x, g_pre, w_in_bf16, ln_g, ln_b, sgu_w, sgu_b, *, seq_len, block_rows, stacked=None):
    n, d = x.shape
    chunk_len = min(seq_len, CHUNK)
    mm_rows = CHUNK if chunk_len == CHUNK else block_rows
    assert n % block_rows == 0 and block_rows % mm_rows == 0 and mm_rows % chunk_len == 0
    assert seq_len % chunk_len == 0
    rep = mm_rows // chunk_len
    wt = jnp.tile(sgu_w[:, :chunk_len, :chunk_len], (1, rep, rep)).reshape(B_GROUPS * mm_rows, mm_rows)
    bias = jnp.repeat(jnp.tile(sgu_b[:, :chunk_len], (1, rep)).T, B_DGROUP, axis=1)

    row = lambda width: pl.BlockSpec((block_rows, width), lambda i: (i, 0))
    const = lambda shape: pl.BlockSpec(shape, lambda i: (0, 0))
    operands = [x, g_pre.reshape(1, d), w_in_bf16, ln_g.reshape(1, B_WIDTH), ln_b.reshape(1, B_WIDTH), wt, bias]
    in_specs = [row(d), const((1, d)), const((d, IN_COLS)), const((1, B_WIDTH)), const((1, B_WIDTH)),
                const((B_GROUPS * mm_rows, mm_rows)), const((mm_rows, B_WIDTH))]
    bf16_shapes = [jax.ShapeDtypeStruct((n, wd), BF16)
                   for wd in (A_WIDTH, A_WIDTH, A_WIDTH, A_WIDTH, B_WIDTH, C_WIDTH, C_WIDTH, C_WIDTH, C_WIDTH)]
    aliases = {}
    own_slab = None
    if stacked is None:
        f32_shapes = [jax.ShapeDtypeStruct((n, wd), F32) for wd in (A_WIDTH, A_WIDTH, C_WIDTH, C_WIDTH, B_WIDTH)]
        f32_specs = [row(s.shape[1]) for s in f32_shapes]
        n_aliased = 0
    else:
        layer, depth, previous = stacked
        assert seq_len % block_rows == 0
        batch, per_seq = n // seq_len, seq_len // block_rows
        f32_shapes = ([jax.ShapeDtypeStruct((depth, n * A_HEADS, 2 * A_DHEAD), F32)] * 2
                      + [jax.ShapeDtypeStruct((depth, batch, C_WIDTH, seq_len), F32)] * 2)
        if previous is None:
            own_slab = layer
            f32_specs = ([pl.BlockSpec((depth, block_rows * A_HEADS, 2 * A_DHEAD), lambda i: (0, i, 0))] * 2
                         + [pl.BlockSpec((depth, None, C_WIDTH, block_rows),
                                         lambda i: (0, i // per_seq, 0, i % per_seq))] * 2)
        else:
            f32_specs = ([pl.BlockSpec((None, block_rows * A_HEADS, 2 * A_DHEAD), lambda i: (layer, i, 0))] * 2
                         + [pl.BlockSpec((None, None, C_WIDTH, block_rows),
                                         lambda i: (layer, i // per_seq, 0, i % per_seq))] * 2)
        n_aliased = 0 if previous is None else len(previous)
        for i in range(n_aliased):
            aliases[len(operands)] = i
            operands.append(previous[i])
            in_specs.append(pl.BlockSpec(memory_space=pl.ANY))
    out_shapes = f32_shapes + bf16_shapes
    outs = pl.pallas_call(
        functools.partial(_in_proj_kernel, chunk_len=chunk_len, mm_rows=mm_rows,
                          cache_layout=stacked is not None, n_aliased=n_aliased, own_slab=own_slab),
        grid=(n // block_rows,),
        in_specs=in_specs,
        out_specs=f32_specs + [row(s.shape[1]) for s in bf16_shapes],
        out_shape=out_shapes,
        input_output_aliases=aliases,
        compiler_params=pltpu.CompilerParams(dimension_semantics=("parallel",),
                                             vmem_limit_bytes=VMEM_LIMIT),
        name="in_proj",
    )(*operands)
    if stacked is not None:
        outs = list(outs[:4]) + [None] + list(outs[4:])
    return tuple(outs)


def _out_proj_kernel(oa_ref, ob_ref, oc_ref, x_ref, w_ref, g_ref, y_ref):
    mix = (_dot(oa_ref[...], w_ref[0:A_WIDTH, :])
           + _dot(ob_ref[...], w_ref[A_WIDTH:A_WIDTH + B_WIDTH, :])
           + _dot(oc_ref[...], w_ref[A_WIDTH + B_WIDTH:, :]))
    y_ref[...] = x_ref[...] + mix * lax.rsqrt(jnp.mean(mix * mix, axis=-1, keepdims=True) + EPS) * g_ref[...]


def _out_proj(o_a, o_b, o_c, x, w_out_bf16, g_post, *, block_rows):
    n, d = x.shape
    row = lambda width: pl.BlockSpec((block_rows, width), lambda i: (i, 0))
    return pl.pallas_call(
        _out_proj_kernel,
        grid=(n // block_rows,),
        in_specs=[row(A_WIDTH), row(B_WIDTH), row(C_WIDTH), row(d),
                  pl.BlockSpec(w_out_bf16.shape, lambda i: (0, 0)),
                  pl.BlockSpec((1, d), lambda i: (0, 0))],
        out_specs=row(d),
        out_shape=jax.ShapeDtypeStruct((n, d), F32),
        compiler_params=pltpu.CompilerParams(dimension_semantics=("parallel",),
                                             vmem_limit_bytes=VMEM_LIMIT),
        name="out_proj",
    )(o_a, o_b, o_c, x, w_out_bf16, g_post.reshape(1, d))


def _split_halves(q):
    lane = lax.broadcasted_iota(jnp.int32, q.shape, 1)
    zero = jnp.zeros_like(q)
    return jnp.concatenate([jnp.where(lane < LANES // 2, q, zero), jnp.where(lane >= LANES // 2, q, zero)], axis=0)


def _lambda(lq1_ref, lk1_ref, lq2_ref, lk2_ref, lam_init):
    s1 = jnp.sum(lq1_ref[...] * lk1_ref[...], axis=-1, keepdims=True)
    s2 = jnp.sum(lq2_ref[...] * lk2_ref[...], axis=-1, keepdims=True)
    return jnp.exp(s1) - jnp.exp(s2) + lam_init


def _diff_finish(acc, l, lam, subln_g, gate, lam_init):
    r = acc.shape[0] // 2
    o = acc[:r] / l[:r] - lam * (acc[r:] / l[r:])
    o = o * lax.rsqrt(jnp.mean(o * o, axis=-1, keepdims=True) + EPS) * subln_g * (1.0 - lam_init)
    return o * gate


def _softmax_update(s, m_prev, l_prev, acc_prev, v_blocks):
    m_new = jnp.maximum(m_prev, jnp.max(s, axis=-1, keepdims=True))
    alpha = jnp.exp2(m_prev - m_new)
    p = jnp.exp2(s - _lane_tile(m_new, s.shape[1]))
    pb = p.astype(BF16)
    pv, start = None, 0
    for v in v_blocks:
        part = _dot(pb[:, start:start + v.shape[0]], v)
        pv = part if pv is None else pv + part
        start += v.shape[0]
    if pv.shape[1] == 2 * LANES:
        row_sum, pv = pv[:, LANES:], pv[:, :LANES]
    else:
        row_sum = jnp.sum(p, axis=-1, keepdims=True)
    return m_new, alpha * l_prev + row_sum, alpha * acc_prev + pv


def _with_ones(v):
    return jnp.concatenate([v, jnp.ones(v.shape, v.dtype)], axis=1)


def _triangle_tables(n_blocks, descending):
    qs, ks = [], []
    for qi in range(n_blocks):
        order = range(qi, -1, -1) if descending else range(qi + 1)
        for ki in order:
            qs.append(qi)
            ks.append(ki)
    return jnp.asarray(qs, jnp.int32), jnp.asarray(ks, jnp.int32)


def _diff_attn_kernel(qtab_ref, ktab_ref, q_ref, k_ref, v_ref, gate_ref, sub_ref,
                      lq1_ref, lk1_ref, lq2_ref, lk2_ref, o_ref,
                      qq_sc, m_sc, l_sc, acc_sc, *, lam_init, row_chunk):
    s_idx = pl.program_id(2)
    qi = qtab_ref[s_idx]
    ki = ktab_ref[s_idx]
    tq = q_ref.shape[0]
    tk = k_ref.shape[0]

    @pl.when(ki == 0)
    def _init():
        qq_sc[...] = _split_halves(q_ref[...])
        m_sc[...] = jnp.full(m_sc.shape, NEG_BIG, F32)
        l_sc[...] = jnp.zeros(l_sc.shape, F32)
        acc_sc[...] = jnp.zeros(acc_sc.shape, F32)

    def step(masked):
        k = k_ref[...]
        v_ones = _with_ones(v_ref[...])
        chunks = []
        for c in range(2 * tq // row_chunk):
            rows = slice(c * row_chunk, (c + 1) * row_chunk)
            q0 = (c * row_chunk) % tq
            width = min(tk, q0 + row_chunk) if masked else tk
            s = _dot_nt(qq_sc[rows], k[:width])
            if masked:
                qpos = q0 + lax.broadcasted_iota(jnp.int32, s.shape, 0)
                kpos = lax.broadcasted_iota(jnp.int32, s.shape, 1)
                s = jnp.where(kpos <= qpos, s, NEG_BIG)
            chunks.append((rows, width, s))
        weights = []
        for rows, width, s in chunks:
            m_prev = m_sc[rows]
            m_new = jnp.maximum(m_prev, jnp.max(s, axis=-1, keepdims=True))
            m_sc[rows] = m_new
            weights.append((jnp.exp2(m_prev - m_new), jnp.exp2(s - _lane_tile(m_new, width)).astype(BF16)))
        for (rows, width, _), (alpha, pb) in zip(chunks, weights):
            pv = _dot(pb, v_ones[:width])
            l_sc[rows] = alpha * l_sc[rows] + pv[:, LANES:]
            acc_sc[rows] = alpha * acc_sc[rows] + pv[:, :LANES]

    @pl.when(ki < qi)
    def _off_diag():
        step(False)

    @pl.when(ki == qi)
    def _diag():
        step(True)
        lam = _lambda(lq1_ref, lk1_ref, lq2_ref, lk2_ref, lam_init)
        o = _diff_finish(acc_sc[...], l_sc[...], lam, sub_ref[...], gate_ref[...].astype(F32), lam_init)
        o_ref[...] = o.astype(o_ref.dtype)


def _diff_attn(q, k, v, gate, subln_g, lams, *, batch, seq_len, block, row_chunk, lam_init):
    n = q.shape[0]
    nb = seq_len // block
    qtab, ktab = _triangle_tables(nb, descending=False)
    qmap = lambda b, h, s, qt, kt: (b * nb + qt[s], h)
    kmap = lambda b, h, s, qt, kt: (b * nb + kt[s], h)
    cmap = lambda b, h, s, qt, kt: (0, 0)
    blk = lambda m: pl.BlockSpec((block, LANES), m)
    lam_spec = pl.BlockSpec((1, A_DHEAD), cmap)
    stat = pltpu.VMEM((2 * block, LANES), F32)
    grid_spec = pltpu.PrefetchScalarGridSpec(
        num_scalar_prefetch=2,
        grid=(batch, A_HEADS, int(qtab.shape[0])),
        in_specs=[blk(qmap), blk(kmap), blk(kmap), blk(qmap), pl.BlockSpec((1, LANES), cmap),
                  lam_spec, lam_spec, lam_spec, lam_spec],
        out_specs=blk(qmap),
        scratch_shapes=[pltpu.VMEM((2 * block, LANES), BF16), stat, stat, stat])
    return pl.pallas_call(
        functools.partial(_diff_attn_kernel, lam_init=lam_init, row_chunk=row_chunk),
        grid_spec=grid_spec,
        out_shape=jax.ShapeDtypeStruct((n, A_WIDTH), BF16),
        compiler_params=pltpu.CompilerParams(dimension_semantics=("parallel", "parallel", "arbitrary"),
                                             vmem_limit_bytes=VMEM_LIMIT),
        name="diff_attn",
    )(qtab, ktab, q, k, v, gate, subln_g.reshape(1, LANES), *[a.reshape(1, A_DHEAD) for a in lams])


def _neg_later(n, with_total):
    j = lax.broadcasted_iota(jnp.int32, (n, n), 0)
    k = lax.broadcasted_iota(jnp.int32, (n, n), 1)
    m = jnp.where(j > k, -1.0, 0.0).astype(BF16)
    return jnp.concatenate([m, jnp.full((n, LANES), -1.0, BF16)], axis=1) if with_total else m


def _sb_attn_kernel(qtab_ref, ktab_ref, q_ref, k_ref, v_ref, gate_ref, o_ref,
                    qq_sc, carry_sc, acc_sc, *, sub, row_chunk):
    s_idx = pl.program_id(2)
    qi = qtab_ref[s_idx]
    ki = ktab_ref[s_idx]
    tq = q_ref.shape[0]
    tk = k_ref.shape[0]

    @pl.when(ki == qi)
    def _init():
        qq_sc[...] = _split_halves(q_ref[...])
        carry_sc[...] = jnp.zeros(carry_sc.shape, F32)
        acc_sc[...] = jnp.zeros(acc_sc.shape, F32)

    def step(masked):
        k = k_ref[...]
        v = v_ref[...]
        tri = _neg_later(sub, with_total=False)
        chunks = []
        for c in range(2 * tq // row_chunk):
            rows = slice(c * row_chunk, (c + 1) * row_chunk)
            q0 = (c * row_chunk) % tq
            width = min(tk, -(-(q0 + row_chunk) // sub) * sub) if masked else tk
            z = _dot_nt(qq_sc[rows], k[:width])
            valid = None
            if masked:
                qpos = q0 + lax.broadcasted_iota(jnp.int32, z.shape, 0)
                kpos = lax.broadcasted_iota(jnp.int32, z.shape, 1)
                valid = kpos < qpos
            chunks.append((rows, width, z, valid))
        prefix = []
        for rows, width, z, valid in chunks:
            sp = _softplus2(z)
            if valid is not None:
                sp = jnp.where(valid, sp, 0.0)
            later = [_dot(sp[:, j * sub:(j + 1) * sub].astype(BF16), tri) for j in range(width // sub)]
            prefix.append((sp, later))
        weights = []
        for (rows, width, z, valid), (sp, later) in zip(chunks, prefix):
            carry = carry_sc[rows]
            ws = [None] * (width // sub)
            for j in range(width // sub - 1, -1, -1):
                cols = slice(j * sub, (j + 1) * sub)
                t = z[:, cols] - sp[:, cols] + later[j] - _lane_tile(carry, sub)
                if valid is not None:
                    t = jnp.where(valid[:, cols], t, NEG_BIG)
                ws[j] = jnp.exp2(t)
                carry = carry + jnp.sum(sp[:, cols], axis=-1, keepdims=True)
            carry_sc[rows] = carry
            weights.append(jnp.concatenate(ws, axis=1).astype(BF16))
        for (rows, width, _, _), wb in zip(chunks, weights):
            acc_sc[rows] += _dot(wb, v[:width])

    @pl.when(ki == qi)
    def _diag():
        step(True)

    @pl.when(ki < qi)
    def _off_diag():
        step(False)

    @pl.when(ki == 0)
    def _finish():
        acc = acc_sc[...]
        lane = lax.broadcasted_iota(jnp.int32, (tq, LANES), 1)
        o = jnp.where(lane < LANES // 2, acc[:tq], acc[tq:])
        o_ref[...] = (o * gate_ref[...].astype(F32)).astype(o_ref.dtype)


def _sb_attn(q, k, v, gate, *, batch, seq_len, block, sub, row_chunk):
    n = q.shape[0]
    nb = seq_len // block
    qtab, ktab = _triangle_tables(nb, descending=True)
    qmap = lambda b, h, s, qt, kt: (b * nb + qt[s], h)
    kmap = lambda b, h, s, qt, kt: (b * nb + kt[s], h)
    blk = lambda m: pl.BlockSpec((block, LANES), m)
    grid_spec = pltpu.PrefetchScalarGridSpec(
        num_scalar_prefetch=2,
        grid=(batch, C_HEADS // 2, int(qtab.shape[0])),
        in_specs=[blk(qmap), blk(kmap), blk(kmap), blk(qmap)],
        out_specs=blk(qmap),
        scratch_shapes=[pltpu.VMEM((2 * block, LANES), BF16), pltpu.VMEM((2 * block, LANES), F32),
                        pltpu.VMEM((2 * block, LANES), F32)])
    return pl.pallas_call(
        functools.partial(_sb_attn_kernel, sub=sub, row_chunk=row_chunk),
        grid_spec=grid_spec,
        out_shape=jax.ShapeDtypeStruct((n, C_WIDTH), BF16),
        compiler_params=pltpu.CompilerParams(dimension_semantics=("parallel", "parallel", "arbitrary"),
                                             vmem_limit_bytes=VMEM_LIMIT),
        name="sb_attn",
    )(qtab, ktab, q, k, v, gate)


def _decode_kernel(pt_ref, qa_ref, kan_ref, van_ref, ga_ref, qc_ref, kcn_ref, vcn_ref, gc_ref, sub_ref,
                   lq1_ref, lk1_ref, lq2_ref, lk2_ref, cdk_ref, cdv_ref, csk_ref, csv_ref, oa_ref, oc_ref,
                   qqa_sc, qqc_sc, m_sc, l_sc, acca_sc, carry_sc, accc_sc,
                   dk_a, dv_a, sk_a, sv_a, dk_b, dv_b, sk_b, sv_b, sems,
                   *, layer, n_pages, pages_per_half, lam_init):
    pph = pages_per_half
    b = pl.program_id(0)
    j = pl.program_id(1)
    n_j = pl.num_programs(1)
    step = b * n_j + j
    last_step = pl.num_programs(0) * n_j - 1
    t = qa_ref.shape[0]
    page = sk_a.shape[2]
    n_pairs = C_HEADS // 2
    tri = _neg_later(page, with_total=True)
    caches = (cdk_ref, cdv_ref, csk_ref, csv_ref)
    buffers = ((dk_a, dv_a, sk_a, sv_a), (dk_b, dv_b, sk_b, sv_b))

    def page_copy(slot, kind, i, page_id):
        return pltpu.make_async_copy(caches[kind].at[layer, page_id], buffers[slot][kind].at[i], sems.at[slot, kind])

    def start_page(slot, seq, first_page, i):
        page_id = pt_ref[seq * n_pages + (n_pages - 1 - (first_page + i))]
        for kind in range(4):
            page_copy(slot, kind, i, page_id).start()

    def wait_slot(slot):
        for i in range(pph):
            for kind in range(4):
                page_copy(slot, kind, i, 0).wait()

    def sb_page(z, valid, carry):
        sp = _softplus2(z)
        if valid is not None:
            sp = jnp.where(valid, sp, 0.0)
        later = _dot(sp.astype(BF16), tri)
        tt = z - sp + later[:, :page] - carry
        if valid is not None:
            tt = jnp.where(valid, tt, NEG_BIG)
        return jnp.exp2(tt), carry - later[:, page:]

    @pl.when(step == 0)
    def _prologue():
        for i in range(pph):
            start_page(0, 0, 0, i)

    @pl.when(j == 0)
    def _first():
        qpos = lax.broadcasted_iota(jnp.int32, (2 * t, page), 0) % t
        kpos = lax.broadcasted_iota(jnp.int32, (2 * t, page), 1)
        kan = kan_ref[...]
        van = van_ref[...]
        for h in range(A_HEADS):
            cols = slice(h * LANES, (h + 1) * LANES)
            qq = _split_halves(qa_ref[:, cols])
            qqa_sc[h] = qq
            s = jnp.where(kpos <= qpos, _dot_nt(qq, kan[:, cols]), NEG_BIG)
            m_sc[h], l_sc[h], acca_sc[h] = _softmax_update(
                s, jnp.full((2 * t, LANES), NEG_BIG, F32), jnp.zeros((2 * t, LANES), F32),
                jnp.zeros((2 * t, LANES), F32), [_with_ones(van[:, cols])])
        kcn = kcn_ref[...]
        vcn = vcn_ref[...]
        for pr in range(n_pairs):
            cols = slice(pr * LANES, (pr + 1) * LANES)
            qq = _split_halves(qc_ref[:, cols])
            qqc_sc[pr] = qq
            w, carry_sc[pr] = sb_page(_dot_nt(qq, kcn[:, cols]), kpos < qpos, jnp.zeros((2 * t, LANES), F32))
            accc_sc[pr] = _dot(w.astype(BF16), vcn[:, cols])

    head_rows = [pl.ds(h, page, stride=A_HEADS) for h in range(A_HEADS)]
    pair_rows = [slice(pr * LANES, (pr + 1) * LANES) for pr in range(n_pairs)]

    def half_step(slot, next_seq, next_first_page):
        dk, dv, sk, sv = buffers[slot]
        for i in range(pph):
            start_page(1 - slot, next_seq, next_first_page, i)
        wait_slot(slot)
        s_heads = [jnp.concatenate([_dot_nt(qqa_sc[h], dk[i, head_rows[h], :].astype(BF16)) for i in range(pph)], axis=1)
                   for h in range(A_HEADS)]
        z_pages = [jnp.concatenate([_dot(qqc_sc[pr], sk[i, pair_rows[pr], :].astype(BF16)) for pr in range(n_pairs)],
                                   axis=0) for i in range(pph)]
        sp_pages = [_softplus2(z) for z in z_pages]
        later_pages = [_dot(sp.astype(BF16), tri) for sp in sp_pages]

        alphas, pbs = [], []
        for h in range(A_HEADS):
            m_prev = m_sc[h]
            m_new = jnp.maximum(m_prev, jnp.max(s_heads[h], axis=-1, keepdims=True))
            alpha = jnp.exp2(m_prev - m_new)
            p = jnp.exp2(s_heads[h] - _lane_tile(m_new, pph * page))
            m_sc[h] = m_new
            l_sc[h] = alpha * l_sc[h] + jnp.sum(p, axis=-1, keepdims=True)
            alphas.append(alpha)
            pbs.append(p.astype(BF16))
        carry = jnp.concatenate([carry_sc[pr] for pr in range(n_pairs)], axis=0)
        wbs = []
        for i in range(pph):
            wbs.append(jnp.exp2(z_pages[i] - sp_pages[i] + later_pages[i][:, :page] - carry).astype(BF16))
            carry = carry - later_pages[i][:, page:]
        for pr in range(n_pairs):
            carry_sc[pr] = carry[pr * 2 * t:(pr + 1) * 2 * t]

        for h in range(A_HEADS):
            pv = _dot(pbs[h][:, :page], dv[0, head_rows[h], :].astype(BF16))
            for i in range(1, pph):
                pv = pv + _dot(pbs[h][:, i * page:(i + 1) * page], dv[i, head_rows[h], :].astype(BF16))
            acca_sc[h] = alphas[h] * acca_sc[h] + pv
        for pr in range(n_pairs):
            rows = slice(pr * 2 * t, (pr + 1) * 2 * t)
            pv = _dot_nt(wbs[0][rows], sv[0, pair_rows[pr], :].astype(BF16))
            for i in range(1, pph):
                pv = pv + _dot_nt(wbs[i][rows], sv[i, pair_rows[pr], :].astype(BF16))
            accc_sc[pr] += pv

    half_step(0, b, (2 * j + 1) * pph)
    nxt = jnp.minimum(step + 1, last_step)
    half_step(1, nxt // n_j, (nxt % n_j) * 2 * pph)

    @pl.when(j == n_j - 1)
    def _finish():
        lam = _lambda(lq1_ref, lk1_ref, lq2_ref, lk2_ref, lam_init)
        lane = lax.broadcasted_iota(jnp.int32, (t, LANES), 1)
        for h in range(A_HEADS):
            cols = slice(h * LANES, (h + 1) * LANES)
            o = _diff_finish(acca_sc[h], l_sc[h], lam, sub_ref[...], ga_ref[:, cols].astype(F32), lam_init)
            oa_ref[:, cols] = o.astype(oa_ref.dtype)
        for pr in range(n_pairs):
            cols = slice(pr * LANES, (pr + 1) * LANES)
            acc = accc_sc[pr]
            o = jnp.where(lane < LANES // 2, acc[:t], acc[t:])
            oc_ref[:, cols] = (o * gc_ref[:, cols].astype(F32)).astype(oc_ref.dtype)

    @pl.when(step == last_step)
    def _drain():
        wait_slot(0)


def _decode(layer, page_table, cache_dk, cache_dv, cache_sk, cache_sv,
            q_a, k_a_new, v_a_new, gate_a, q_c, k_c_new, v_c_new, gate_c, subln_g, lams,
            *, dec_batch, dec_seq, pages_per_half, lam_init):
    n_pages = page_table.shape[1]
    depth, n_pool, page = cache_dk.shape[:3]
    pph = pages_per_half
    assert n_pages % (2 * pph) == 0
    n_steps = n_pages // (2 * pph)
    cdk = cache_dk.reshape(depth, n_pool, page * A_HEADS, 2 * A_DHEAD)
    cdv = cache_dv.reshape(depth, n_pool, page * A_HEADS, 2 * A_DHEAD)
    csk = jnp.transpose(cache_sk, (0, 1, 3, 4, 2)).reshape(depth, n_pool, C_WIDTH, page)
    csv = jnp.transpose(cache_sv, (0, 1, 3, 4, 2)).reshape(depth, n_pool, C_WIDTH, page)
    pt_flat = page_table.reshape(-1)

    per_seq = lambda a: a.reshape(dec_batch, dec_seq, a.shape[-1])
    pad_page = lambda a: jnp.pad(per_seq(a), ((0, 0), (0, page - dec_seq), (0, 0)))
    seq_spec = lambda rows, width: pl.BlockSpec((None, rows, width), lambda b, j, pt: (b, 0, 0))
    const_spec = lambda width: pl.BlockSpec((1, width), lambda b, j, pt: (0, 0))
    hbm_spec = pl.BlockSpec(memory_space=pl.ANY)

    in_specs = ([seq_spec(dec_seq, A_WIDTH), seq_spec(page, A_WIDTH), seq_spec(page, A_WIDTH),
                 seq_spec(dec_seq, A_WIDTH),
                 seq_spec(dec_seq, C_WIDTH), seq_spec(page, C_WIDTH), seq_spec(page, C_WIDTH),
                 seq_spec(dec_seq, C_WIDTH),
                 const_spec(LANES)] + [const_spec(A_DHEAD)] * 4 + [hbm_spec] * 4)
    t2 = 2 * dec_seq
    stat_a = pltpu.VMEM((A_HEADS, t2, LANES), F32)
    stat_c = pltpu.VMEM((C_HEADS // 2, t2, LANES), F32)
    page_bufs = [pltpu.VMEM((pph, page * A_HEADS, 2 * A_DHEAD), F32)] * 2 + [pltpu.VMEM((pph, C_WIDTH, page), F32)] * 2
    grid_spec = pltpu.PrefetchScalarGridSpec(
        num_scalar_prefetch=1,
        grid=(dec_batch, n_steps),
        in_specs=in_specs,
        out_specs=[seq_spec(dec_seq, A_WIDTH), seq_spec(dec_seq, C_WIDTH)],
        scratch_shapes=[pltpu.VMEM((A_HEADS, t2, LANES), BF16), pltpu.VMEM((C_HEADS // 2, t2, LANES), BF16),
                        stat_a, stat_a, stat_a, stat_c, stat_c] + page_bufs * 2
                       + [pltpu.SemaphoreType.DMA((2, 4))])
    o_a, o_c = pl.pallas_call(
        functools.partial(_decode_kernel, layer=layer, n_pages=n_pages, pages_per_half=pph, lam_init=lam_init),
        grid_spec=grid_spec,
        out_shape=[jax.ShapeDtypeStruct((dec_batch, dec_seq, A_WIDTH), F32),
                   jax.ShapeDtypeStruct((dec_batch, dec_seq, C_WIDTH), F32)],
        compiler_params=pltpu.CompilerParams(dimension_semantics=("arbitrary", "arbitrary"),
                                             vmem_limit_bytes=VMEM_LIMIT),
        name="decode_attn",
    )(pt_flat, per_seq(q_a), pad_page(k_a_new), pad_page(v_a_new), per_seq(gate_a),
      per_seq(q_c), pad_page(k_c_new), pad_page(v_c_new), per_seq(gate_c),
      subln_g.reshape(1, LANES), *[a.reshape(1, A_DHEAD) for a in lams], cdk, cdv, csk, csv)
    n = dec_batch * dec_seq
    return o_a.reshape(n, A_WIDTH).astype(BF16), o_c.reshape(n, C_WIDTH).astype(BF16)


PROMPT_ROWS = 512
ATTN_BLOCK = 1024
ATTN_ROW_CHUNK = 256
SB_SUB = 256
PAGES_PER_HALF = 8


def kernel(x_prompt, x_sample, cache_diff_k, cache_diff_v, cache_sb_k, cache_sb_v, page_table, w_in, w_out, norm_pre_g, norm_post_g, lambda_q1, lambda_k1, lambda_q2, lambda_k2, diff_subln_g, sgu_ln_g, sgu_ln_b, sgu_w, sgu_b):
    batch, seq, d = x_prompt.shape
    dec_batch, dec_seq, _ = x_sample.shape
    depth = w_in.shape[0]
    w_in_b = w_in.astype(BF16)
    w_out_b = w_out.astype(BF16)
    hp = x_prompt.reshape(batch * seq, d)
    hs = x_sample.reshape(dec_batch * dec_seq, d)
    block = min(ATTN_BLOCK, seq)
    sub = min(SB_SUB, block)
    row_chunk = min(ATTN_ROW_CHUNK, block)
    prompt_rows = min(PROMPT_ROWS, seq)
    new_kv_prompt = None
    outs_s = [[] for _ in range(5)]
    for l in range(depth):
        lam_init = 0.8 - 0.6 * math.exp(-0.3 * l)
        lams = (lambda_q1[l], lambda_k1[l], lambda_q2[l], lambda_k2[l])

        (ak, av, ck, cv, _, qa, ka, va, ga, ob, qc, kc, vc, gc) = _in_proj(
            hp, norm_pre_g[l], w_in_b[l], sgu_ln_g[l], sgu_ln_b[l], sgu_w[l], sgu_b[l],
            seq_len=seq, block_rows=prompt_rows, stacked=(l, depth, new_kv_prompt))
        new_kv_prompt = (ak, av, ck, cv)
        oa = _diff_attn(qa, ka, va, ga, diff_subln_g[l], lams, batch=batch, seq_len=seq, block=block,
                        row_chunk=row_chunk, lam_init=lam_init)
        oc = _sb_attn(qc, kc, vc, gc, batch=batch, seq_len=seq, block=block, sub=sub, row_chunk=row_chunk)
        hp = _out_proj(oa, ob, oc, hp, w_out_b[l], norm_post_g[l], block_rows=prompt_rows)

        (ak, av, ck, cv, vn, qa, ka, va, ga, ob, qc, kc, vc, gc) = _in_proj(
            hs, norm_pre_g[l], w_in_b[l], sgu_ln_g[l], sgu_ln_b[l], sgu_w[l], sgu_b[l],
            seq_len=dec_seq, block_rows=dec_batch * dec_seq)
        oa, oc = _decode(l, page_table, cache_diff_k, cache_diff_v, cache_sb_k, cache_sb_v,
                         qa, ka, va, ga, qc, kc, vc, gc, diff_subln_g[l], lams,
                         dec_batch=dec_batch, dec_seq=dec_seq, pages_per_half=PAGES_PER_HALF,
                         lam_init=lam_init)
        hs = _out_proj(oa, ob, oc, hs, w_out_b[l], norm_post_g[l], block_rows=dec_batch * dec_seq)
        for dst, a, heads in zip(outs_s, (ak, av, ck, cv), (A_HEADS, A_HEADS, C_HEADS, C_HEADS)):
            dst.append(a.reshape(dec_batch, dec_seq, heads, a.shape[-1] // heads))
        outs_s[4].append(vn.reshape(dec_batch, dec_seq, B_WIDTH))

    ak, av, ck, cv = new_kv_prompt
    outs_p = [a.reshape(depth, batch, seq, A_HEADS, 2 * A_DHEAD) for a in (ak, av)]
    outs_p += [jnp.transpose(a.reshape(depth, batch, C_HEADS, C_DHEAD, seq), (0, 1, 4, 2, 3)) for a in (ck, cv)]
    stack = lambda xs: jnp.stack(xs, 0)
    return (hp.reshape(batch, seq, d), hs.reshape(dec_batch, dec_seq, d),
            *outs_p, *[stack(o) for o in outs_s])
```

```python
import functools
import math

import jax
import jax.numpy as jnp
from jax import lax
from jax.experimental import pallas as pl
from jax.experimental.pallas import tpu as pltpu

A_HEADS = 4
A_DHEAD = 64
A_WIDTH = A_HEADS * 2 * A_DHEAD
B_GROUPS = 4
B_DGROUP = 64
B_WIDTH = B_GROUPS * B_DGROUP
CHUNK = 128
C_HEADS = 4
C_DHEAD = 64
C_WIDTH = C_HEADS * C_DHEAD
IN_COLS = 4 * A_WIDTH + 3 * B_WIDTH + 4 * C_WIDTH
EPS = 1e-6
LANES = 128
NEG_BIG = -1e30
LOG2E = 1.4426950408889634
VMEM_LIMIT = 56 * 1024 * 1024

_OFF = {}
_o = 0
for _name, _w in (("a_q", A_WIDTH), ("a_k", A_WIDTH), ("a_v", A_WIDTH), ("a_g", A_WIDTH),
                  ("b_u", B_WIDTH), ("b_v", B_WIDTH), ("b_g", B_WIDTH),
                  ("c_q", C_WIDTH), ("c_k", C_WIDTH), ("c_v", C_WIDTH), ("c_g", C_WIDTH)):
    _OFF[_name] = (_o, _w)
    _o += _w

BF16 = jnp.bfloat16
F32 = jnp.float32


def _silu(x):
    return x / (1.0 + jnp.exp(-x))


def _softplus2(z):
    neg_abs = lax.bitcast_convert_type(lax.bitcast_convert_type(z, jnp.uint32) | jnp.uint32(0x80000000), F32)
    return jnp.maximum(z, 0.0) + jnp.log2(1.0 + jnp.exp2(neg_abs))


def _lane_tile(x, width):
    return x if width == LANES else jnp.concatenate([x] * (width // LANES), axis=1)


def _dot_nt(a, b):
    return lax.dot_general(a, b, (((1,), (1,)), ((), ())), preferred_element_type=F32)


def _dot(a, b):
    return jnp.dot(a, b, preferred_element_type=F32)


def _in_proj_kernel(x_ref, gpre_ref, w_ref, lng_ref, lnb_ref, wt_ref, bias_ref, *rest,
                    chunk_len, mm_rows, cache_layout, n_aliased, own_slab):
    if cache_layout:
        kv_refs = rest[n_aliased:4 + n_aliased]
        if own_slab is not None:
            for ref in kv_refs:
                for slab in range(ref.shape[0]):
                    if slab != own_slab:
                        ref[slab] = jnp.zeros(ref.shape[1:], ref.dtype)
            kv_refs = [ref.at[own_slab] for ref in kv_refs]
        ak_ref, av_ref, ck_ref, cv_ref = kv_refs
        vn_ref = None
        qa_ref, ka_ref, va_ref, ga_ref, ob_ref, qc_ref, kc_ref, vc_ref, gc_ref = rest[4 + n_aliased:]
    else:
        ak_ref, av_ref, ck_ref, cv_ref, vn_ref = rest[:5]
        qa_ref, ka_ref, va_ref, ga_ref, ob_ref, qc_ref, kc_ref, vc_ref, gc_ref = rest[5:]
    tm = x_ref.shape[0]
    x = x_ref[...]
    h = x * lax.rsqrt(jnp.mean(x * x, axis=-1, keepdims=True) + EPS) * gpre_ref[...]
    hb = h.astype(BF16)

    def proj(name):
        off, width = _OFF[name]
        return _dot(hb, w_ref[:, off:off + width])

    def store_a(ref, val):
        if cache_layout:
            for head in range(A_HEADS):
                ref[pl.ds(head, tm, stride=A_HEADS), :] = val[:, head * LANES:(head + 1) * LANES]
        else:
            ref[...] = val

    a_q = proj("a_q")
    qa_ref[...] = (a_q * (A_DHEAD ** -0.5 * LOG2E)).astype(BF16)
    a_k = proj("a_k")
    store_a(ak_ref, a_k)
    ka_ref[...] = a_k.astype(BF16)
    a_v = proj("a_v")
    store_a(av_ref, a_v)
    va_ref[...] = a_v.astype(BF16)
    ga_ref[...] = _silu(proj("a_g")).astype(BF16)
    c_q = proj("c_q")
    qc_ref[...] = (c_q * (C_DHEAD ** -0.5 * LOG2E)).astype(BF16)
    c_k = proj("c_k")
    kc_ref[...] = c_k.astype(BF16)
    c_v = proj("c_v")
    vc_ref[...] = c_v.astype(BF16)
    if cache_layout:
        ck_ref[...] = c_k.T
        cv_ref[...] = c_v.T
    else:
        ck_ref[...] = c_k
        cv_ref[...] = c_v
    gc_ref[...] = _silu(proj("c_g")).astype(BF16)

    b_v = proj("b_v")
    mu = jnp.mean(b_v, axis=-1, keepdims=True)
    cen = b_v - mu
    var = jnp.mean(cen * cen, axis=-1, keepdims=True)
    vn = cen * lax.rsqrt(var + EPS) * lng_ref[...] + lnb_ref[...]
    if vn_ref is not None:
        vn_ref[...] = vn
    vnb = vn.astype(BF16)

    r = lax.broadcasted_iota(jnp.int32, (B_GROUPS * mm_rows, mm_rows), 0) % mm_rows
    c = lax.broadcasted_iota(jnp.int32, (B_GROUPS * mm_rows, mm_rows), 1)
    keep = (r // chunk_len == c // chunk_len) & (c % chunk_len <= r % chunk_len)
    wt = jnp.where(keep, wt_ref[...], 0.0).astype(BF16)
    grp = lax.broadcasted_iota(jnp.int32, (mm_rows, B_WIDTH), 1) // B_DGROUP
    b_u = proj("b_u")
    gate_b = _silu(proj("b_g"))
    for i in range(tm // mm_rows):
        rows = slice(i * mm_rows, (i + 1) * mm_rows)
        full = _dot(wt, vnb[rows])
        mixed = bias_ref[...]
        for g in range(B_GROUPS):
            mixed = mixed + jnp.where(grp == g, full[g * mm_rows:(g + 1) * mm_rows], 0.0)
        ob_ref[rows, :] = (b_u[rows] * mixed * gate_b[rows]).astype(BF16)


def _in_proj(x, g_pre, w_in_bf16, ln_g, ln_b, sgu_w, sgu_b, *, seq_len, block_rows, stacked=None):
    n, d = x.shape
    chunk_len = min(seq_len, CHUNK)
    mm_rows = CHUNK if chunk_len == CHUNK else block_rows
    assert n % block_rows == 0 and block_rows % mm_rows == 0 and mm_rows % chunk_len == 0
    assert seq_len % chunk_len == 0
    rep = mm_rows // chunk_len
    wt = jnp.tile(sgu_w[:, :chunk_len, :chunk_len], (1, rep, rep)).reshape(B_GROUPS * mm_rows, mm_rows)
    bias = jnp.repeat(jnp.tile(sgu_b[:, :chunk_len], (1, rep)).T, B_DGROUP, axis=1)

    row = lambda width: pl.BlockSpec((block_rows, width), lambda i: (i, 0))
    const = lambda shape: pl.BlockSpec(shape, lambda i: (0, 0))
    operands = [x, g_pre.reshape(1, d), w_in_bf16, ln_g.reshape(1, B_WIDTH), ln_b.reshape(1, B_WIDTH), wt, bias]
    in_specs = [row(d), const((1, d)), const((d, IN_COLS)), const((1, B_WIDTH)), const((1, B_WIDTH)),
                const((B_GROUPS * mm_rows, mm_rows)), const((mm_rows, B_WIDTH))]
    bf16_shapes = [jax.ShapeDtypeStruct((n, wd), BF16)
                   for wd in (A_WIDTH, A_WIDTH, A_WIDTH, A_WIDTH, B_WIDTH, C_WIDTH, C_WIDTH, C_WIDTH, C_WIDTH)]
    aliases = {}
    own_slab = None
    if stacked is None:
        f32_shapes = [jax.ShapeDtypeStruct((n, wd), F32) for wd in (A_WIDTH, A_WIDTH, C_WIDTH, C_WIDTH, B_WIDTH)]
        f32_specs = [row(s.shape[1]) for s in f32_shapes]
        n_aliased = 0
    else:
        layer, depth, previous = stacked
        assert seq_len % block_rows == 0
        batch, per_seq = n // seq_len, seq_len // block_rows
        f32_shapes = ([jax.ShapeDtypeStruct((depth, n * A_HEADS, 2 * A_DHEAD), F32)] * 2
                      + [jax.ShapeDtypeStruct((depth, batch, C_WIDTH, seq_len), F32)] * 2)
        if previous is None:
            own_slab = layer
            f32_specs = ([pl.BlockSpec((depth, block_rows * A_HEADS, 2 * A_DHEAD), lambda i: (0, i, 0))] * 2
                         + [pl.BlockSpec((depth, None, C_WIDTH, block_rows),
                                         lambda i: (0, i // per_seq, 0, i % per_seq))] * 2)
        else:
            f32_specs = ([pl.BlockSpec((None, block_rows * A_HEADS, 2 * A_DHEAD), lambda i: (layer, i, 0))] * 2
                         + [pl.BlockSpec((None, None, C_WIDTH, block_rows),
                                         lambda i: (layer, i // per_seq, 0, i % per_seq))] * 2)
        n_aliased = 0 if previous is None else len(previous)
        for i in range(n_aliased):
            aliases[len(operands)] = i
            operands.append(previous[i])
            in_specs.append(pl.BlockSpec(memory_space=pl.ANY))
    out_shapes = f32_shapes + bf16_shapes
    outs = pl.pallas_call(
        functools.partial(_in_proj_kernel, chunk_len=chunk_len, mm_rows=mm_rows,
                          cache_layout=stacked is not None, n_aliased=n_aliased, own_slab=own_slab),
        grid=(n // block_rows,),
        in_specs=in_specs,
        out_specs=f32_specs + [row(s.shape[1]) for s in bf16_shapes],
        out_shape=out_shapes,
        input_output_aliases=aliases,
        compiler_params=pltpu.CompilerParams(dimension_semantics=("parallel",),
                                             vmem_limit_bytes=VMEM_LIMIT),
        name="in_proj",
    )(*operands)
    if stacked is not None:
        outs = list(outs[:4]) + [None] + list(outs[4:])
    return tuple(outs)


def _out_proj_kernel(oa_ref, ob_ref, oc_ref, x_ref, w_ref, g_ref, y_ref):
    mix = (_dot(oa_ref[...], w_ref[0:A_WIDTH, :])
           + _dot(ob_ref[...], w_ref[A_WIDTH:A_WIDTH + B_WIDTH, :])
           + _dot(oc_ref[...], w_ref[A_WIDTH + B_WIDTH:, :]))
    y_ref[...] = x_ref[...] + mix * lax.rsqrt(jnp.mean(mix * mix, axis=-1, keepdims=True) + EPS) * g_ref[...]


def _out_proj(o_a, o_b, o_c, x, w_out_bf16, g_post, *, block_rows):
    n, d = x.shape
    row = lambda width: pl.BlockSpec((block_rows, width), lambda i: (i, 0))
    return pl.pallas_call(
        _out_proj_kernel,
        grid=(n // block_rows,),
        in_specs=[row(A_WIDTH), row(B_WIDTH), row(C_WIDTH), row(d),
                  pl.BlockSpec(w_out_bf16.shape, lambda i: (0, 0)),
                  pl.BlockSpec((1, d), lambda i: (0, 0))],
        out_specs=row(d),
        out_shape=jax.ShapeDtypeStruct((n, d), F32),
        compiler_params=pltpu.CompilerParams(dimension_semantics=("parallel",),
                                             vmem_limit_bytes=VMEM_LIMIT),
        name="out_proj",
    )(o_a, o_b, o_c, x, w_out_bf16, g_post.reshape(1, d))


def _split_halves(q):
    lane = lax.broadcasted_iota(jnp.int32, q.shape, 1)
    zero = jnp.zeros_like(q)
    return jnp.concatenate([jnp.where(lane < LANES // 2, q, zero), jnp.where(lane >= LANES // 2, q, zero)], axis=0)


def _lambda(lq1_ref, lk1_ref, lq2_ref, lk2_ref, lam_init):
    s1 = jnp.sum(lq1_ref[...] * lk1_ref[...], axis=-1, keepdims=True)
    s2 = jnp.sum(lq2_ref[...] * lk2_ref[...], axis=-1, keepdims=True)
    return jnp.exp(s1) - jnp.exp(s2) + lam_init


def _diff_finish(acc, l, lam, subln_g, gate, lam_init):
    r = acc.shape[0] // 2
    o = acc[:r] / l[:r] - lam * (acc[r:] / l[r:])
    o = o * lax.rsqrt(jnp.mean(o * o, axis=-1, keepdims=True) + EPS) * subln_g * (1.0 - lam_init)
    return o * gate


def _softmax_update(s, m_prev, l_prev, acc_prev, v_blocks):
    m_new = jnp.maximum(m_prev, jnp.max(s, axis=-1, keepdims=True))
    alpha = jnp.exp2(m_prev - m_new)
    p = jnp.exp2(s - _lane_tile(m_new, s.shape[1]))
    pb = p.astype(BF16)
    pv, start = None, 0
    for v in v_blocks:
        part = _dot(pb[:, start:start + v.shape[0]], v)
        pv = part if pv is None else pv + part
        start += v.shape[0]
    if pv.shape[1] == 2 * LANES:
        row_sum, pv = pv[:, LANES:], pv[:, :LANES]
    else:
        row_sum = jnp.sum(p, axis=-1, keepdims=True)
    return m_new, alpha * l_prev + row_sum, alpha * acc_prev + pv


def _with_ones(v):
    return jnp.concatenate([v, jnp.ones(v.shape, v.dtype)], axis=1)


def _triangle_tables(n_blocks, descending):
    qs, ks = [], []
    for qi in range(n_blocks):
        order = range(qi, -1, -1) if descending else range(qi + 1)
        for ki in order:
            qs.append(qi)
            ks.append(ki)
    return jnp.asarray(qs, jnp.int32), jnp.asarray(ks, jnp.int32)


def _diff_attn_kernel(qtab_ref, ktab_ref, q_ref, k_ref, v_ref, gate_ref, sub_ref,
                      lq1_ref, lk1_ref, lq2_ref, lk2_ref, o_ref,
                      qq_sc, m_sc, l_sc, acc_sc, *, lam_init, row_chunk):
    s_idx = pl.program_id(2)
    qi = qtab_ref[s_idx]
    ki = ktab_ref[s_idx]
    tq = q_ref.shape[0]
    tk = k_ref.shape[0]

    @pl.when(ki == 0)
    def _init():
        qq_sc[...] = _split_halves(q_ref[...])
        m_sc[...] = jnp.full(m_sc.shape, NEG_BIG, F32)
        l_sc[...] = jnp.zeros(l_sc.shape, F32)
        acc_sc[...] = jnp.zeros(acc_sc.shape, F32)

    def step(masked):
        k = k_ref[...]
        v_ones = _with_ones(v_ref[...])
        chunks = []
        for c in range(2 * tq // row_chunk):
            rows = slice(c * row_chunk, (c + 1) * row_chunk)
            q0 = (c * row_chunk) % tq
            width = min(tk, q0 + row_chunk) if masked else tk
            s = _dot_nt(qq_sc[rows], k[:width])
            if masked:
                qpos = q0 + lax.broadcasted_iota(jnp.int32, s.shape, 0)
                kpos = lax.broadcasted_iota(jnp.int32, s.shape, 1)
                s = jnp.where(kpos <= qpos, s, NEG_BIG)
            chunks.append((rows, width, s))
        weights = []
        for rows, width, s in chunks:
            m_prev = m_sc[rows]
            m_new = jnp.maximum(m_prev, jnp.max(s, axis=-1, keepdims=True))
            m_sc[rows] = m_new
            weights.append((jnp.exp2(m_prev - m_new), jnp.exp2(s - _lane_tile(m_new, width)).astype(BF16)))
        for (rows, width, _), (alpha, pb) in zip(chunks, weights):
            pv = _dot(pb, v_ones[:width])
            l_sc[rows] = alpha * l_sc[rows] + pv[:, LANES:]
            acc_sc[rows] = alpha * acc_sc[rows] + pv[:, :LANES]

    @pl.when(ki < qi)
    def _off_diag():
        step(False)

    @pl.when(ki == qi)
    def _diag():
        step(True)
        lam = _lambda(lq1_ref, lk1_ref, lq2_ref, lk2_ref, lam_init)
        o = _diff_finish(acc_sc[...], l_sc[...], lam, sub_ref[...], gate_ref[...].astype(F32), lam_init)
        o_ref[...] = o.astype(o_ref.dtype)


def _diff_attn(q, k, v, gate, subln_g, lams, *, batch, seq_len, block, row_chunk, lam_init):
    n = q.shape[0]
    nb = seq_len // block
    qtab, ktab = _triangle_tables(nb, descending=False)
    qmap = lambda b, h, s, qt, kt: (b * nb + qt[s], h)
    kmap = lambda b, h, s, qt, kt: (b * nb + kt[s], h)
    cmap = lambda b, h, s, qt, kt: (0, 0)
    blk = lambda m: pl.BlockSpec((block, LANES), m)
    lam_spec = pl.BlockSpec((1, A_DHEAD), cmap)
    stat = pltpu.VMEM((2 * block, LANES), F32)
    grid_spec = pltpu.PrefetchScalarGridSpec(
        num_scalar_prefetch=2,
        grid=(batch, A_HEADS, int(qtab.shape[0])),
        in_specs=[blk(qmap), blk(kmap), blk(kmap), blk(qmap), pl.BlockSpec((1, LANES), cmap),
                  lam_spec, lam_spec, lam_spec, lam_spec],
        out_specs=blk(qmap),
        scratch_shapes=[pltpu.VMEM((2 * block, LANES), BF16), stat, stat, stat])
    return pl.pallas_call(
        functools.partial(_diff_attn_kernel, lam_init=lam_init, row_chunk=row_chunk),
        grid_spec=grid_spec,
        out_shape=jax.ShapeDtypeStruct((n, A_WIDTH), BF16),
        compiler_params=pltpu.CompilerParams(dimension_semantics=("parallel", "parallel", "arbitrary"),
                                             vmem_limit_bytes=VMEM_LIMIT),
        name="diff_attn",
    )(qtab, ktab, q, k, v, gate, subln_g.reshape(1, LANES), *[a.reshape(1, A_DHEAD) for a in lams])


def _neg_later(n, with_total):
    j = lax.broadcasted_iota(jnp.int32, (n, n), 0)
    k = lax.broadcasted_iota(jnp.int32, (n, n), 1)
    m = jnp.where(j > k, -1.0, 0.0).astype(BF16)
    return jnp.concatenate([m, jnp.full((n, LANES), -1.0, BF16)], axis=1) if with_total else m


def _sb_attn_kernel(qtab_ref, ktab_ref, q_ref, k_ref, v_ref, gate_ref, o_ref,
                    qq_sc, carry_sc, acc_sc, *, sub, row_chunk):
    s_idx = pl.program_id(2)
    qi = qtab_ref[s_idx]
    ki = ktab_ref[s_idx]
    tq = q_ref.shape[0]
    tk = k_ref.shape[0]

    @pl.when(ki == qi)
    def _init():
        qq_sc[...] = _split_halves(q_ref[...])
        carry_sc[...] = jnp.zeros(carry_sc.shape, F32)
        acc_sc[...] = jnp.zeros(acc_sc.shape, F32)

    def step(masked):
        k = k_ref[...]
        v = v_ref[...]
        tri = _neg_later(sub, with_total=False)
        chunks = []
        for c in range(2 * tq // row_chunk):
            rows = slice(c * row_chunk, (c + 1) * row_chunk)
            q0 = (c * row_chunk) % tq
            width = min(tk, -(-(q0 + row_chunk) // sub) * sub) if masked else tk
            z = _dot_nt(qq_sc[rows], k[:width])
            valid = None
            if masked:
                qpos = q0 + lax.broadcasted_iota(jnp.int32, z.shape, 0)
                kpos = lax.broadcasted_iota(jnp.int32, z.shape, 1)
                valid = kpos < qpos
            chunks.append((rows, width, z, valid))
        prefix = []
        for rows, width, z, valid in chunks:
            sp = _softplus2(z)
            if valid is not None:
                sp = jnp.where(valid, sp, 0.0)
            later = [_dot(sp[:, j * sub:(j + 1) * sub].astype(BF16), tri) for j in range(width // sub)]
            prefix.append((sp, later))
        weights = []
        for (rows, width, z, valid), (sp, later) in zip(chunks, prefix):
            carry = carry_sc[rows]
            ws = [None] * (width // sub)
            for j in range(width // sub - 1, -1, -1):
                cols = slice(j * sub, (j + 1) * sub)
                t = z[:, cols] - sp[:, cols] + later[j] - _lane_tile(carry, sub)
                if valid is not None:
                    t = jnp.where(valid[:, cols], t, NEG_BIG)
                ws[j] = jnp.exp2(t)
                carry = carry + jnp.sum(sp[:, cols], axis=-1, keepdims=True)
            carry_sc[rows] = carry
            weights.append(jnp.concatenate(ws, axis=1).astype(BF16))
        for (rows, width, _, _), wb in zip(chunks, weights):
            acc_sc[rows] += _dot(wb, v[:width])

    @pl.when(ki == qi)
    def _diag():
        step(True)

    @pl.when(ki < qi)
    def _off_diag():
        step(False)

    @pl.when(ki == 0)
    def _finish():
        acc = acc_sc[...]
        lane = lax.broadcasted_iota(jnp.int32, (tq, LANES), 1)
        o = jnp.where(lane < LANES // 2, acc[:tq], acc[tq:])
        o_ref[...] = (o * gate_ref[...].astype(F32)).astype(o_ref.dtype)


def _sb_attn(q, k, v, gate, *, batch, seq_len, block, sub, row_chunk):
    n = q.shape[0]
    nb = seq_len // block
    qtab, ktab = _triangle_tables(nb, descending=True)
    qmap = lambda b, h, s, qt, kt: (b * nb + qt[s], h)
    kmap = lambda b, h, s, qt, kt: (b * nb + kt[s], h)
    blk = lambda m: pl.BlockSpec((block, LANES), m)
    grid_spec = pltpu.PrefetchScalarGridSpec(
        num_scalar_prefetch=2,
        grid=(batch, C_HEADS // 2, int(qtab.shape[0])),
        in_specs=[blk(qmap), blk(kmap), blk(kmap), blk(qmap)],
        out_specs=blk(qmap),
        scratch_shapes=[pltpu.VMEM((2 * block, LANES), BF16), pltpu.VMEM((2 * block, LANES), F32),
                        pltpu.VMEM((2 * block, LANES), F32)])
    return pl.pallas_call(
        functools.partial(_sb_attn_kernel, sub=sub, row_chunk=row_chunk),
        grid_spec=grid_spec,
        out_shape=jax.ShapeDtypeStruct((n, C_WIDTH), BF16),
        compiler_params=pltpu.CompilerParams(dimension_semantics=("parallel", "parallel", "arbitrary"),
                                             vmem_limit_bytes=VMEM_LIMIT),
        name="sb_attn",
    )(qtab, ktab, q, k, v, gate)


def _decode_kernel(pt_ref, qa_ref, kan_ref, van_ref, ga_ref, qc_ref, kcn_ref, vcn_ref, gc_ref, sub_ref,
                   lq1_ref, lk1_ref, lq2_ref, lk2_ref, cdk_ref, cdv_ref, csk_ref, csv_ref, oa_ref, oc_ref,
                   qqa_sc, qqc_sc, m_sc, l_sc, acca_sc, carry_sc, accc_sc,
                   dk_a, dv_a, sk_a, sv_a, dk_b, dv_b, sk_b, sv_b, sems,
                   *, layer, n_pages, pages_per_half, lam_init):
    pph = pages_per_half
    b = pl.program_id(0)
    j = pl.program_id(1)
    n_j = pl.num_programs(1)
    step = b * n_j + j
    last_step = pl.num_programs(0) * n_j - 1
    t = qa_ref.shape[0]
    page = sk_a.shape[2]
    n_pairs = C_HEADS // 2
    tri = _neg_later(page, with_total=True)
    caches = (cdk_ref, cdv_ref, csk_ref, csv_ref)
    buffers = ((dk_a, dv_a, sk_a, sv_a), (dk_b, dv_b, sk_b, sv_b))

    def page_copy(slot, kind, i, page_id):
        return pltpu.make_async_copy(caches[kind].at[layer, page_id], buffers[slot][kind].at[i], sems.at[slot, kind])

    def start_page(slot, seq, first_page, i):
        page_id = pt_ref[seq * n_pages + (n_pages - 1 - (first_page + i))]
        for kind in range(4):
            page_copy(slot, kind, i, page_id).start()

    def wait_slot(slot):
        for i in range(pph):
            for kind in range(4):
                page_copy(slot, kind, i, 0).wait()

    def sb_page(z, valid, carry):
        sp = _softplus2(z)
        if valid is not None:
            sp = jnp.where(valid, sp, 0.0)
        later = _dot(sp.astype(BF16), tri)
        tt = z - sp + later[:, :page] - carry
        if valid is not None:
            tt = jnp.where(valid, tt, NEG_BIG)
        return jnp.exp2(tt), carry - later[:, page:]

    @pl.when(step == 0)
    def _prologue():
        for i in range(pph):
            start_page(0, 0, 0, i)

    @pl.when(j == 0)
    def _first():
        qpos = lax.broadcasted_iota(jnp.int32, (2 * t, page), 0) % t
        kpos = lax.broadcasted_iota(jnp.int32, (2 * t, page), 1)
        kan = kan_ref[...]
        van = van_ref[...]
        for h in range(A_HEADS):
            cols = slice(h * LANES, (h + 1) * LANES)
            qq = _split_halves(qa_ref[:, cols])
            qqa_sc[h] = qq
            s = jnp.where(kpos <= qpos, _dot_nt(qq, kan[:, cols]), NEG_BIG)
            m_sc[h], l_sc[h], acca_sc[h] = _softmax_update(
                s, jnp.full((2 * t, LANES), NEG_BIG, F32), jnp.zeros((2 * t, LANES), F32),
                jnp.zeros((2 * t, LANES), F32), [_with_ones(van[:, cols])])
        kcn = kcn_ref[...]
        vcn = vcn_ref[...]
        for pr in range(n_pairs):
            cols = slice(pr * LANES, (pr + 1) * LANES)
            qq = _split_halves(qc_ref[:, cols])
            qqc_sc[pr] = qq
            w, carry_sc[pr] = sb_page(_dot_nt(qq, kcn[:, cols]), kpos < qpos, jnp.zeros((2 * t, LANES), F32))
            accc_sc[pr] = _dot(w.astype(BF16), vcn[:, cols])

    head_rows = [pl.ds(h, page, stride=A_HEADS) for h in range(A_HEADS)]
    pair_rows = [slice(pr * LANES, (pr + 1) * LANES) for pr in range(n_pairs)]

    def half_step(slot, next_seq, next_first_page):
        dk, dv, sk, sv = buffers[slot]
        for i in range(pph):
            start_page(1 - slot, next_seq, next_first_page, i)
        wait_slot(slot)
        s_heads = [jnp.concatenate([_dot_nt(qqa_sc[h], dk[i, head_rows[h], :].astype(BF16)) for i in range(pph)], axis=1)
                   for h in range(A_HEADS)]
        z_pages = [jnp.concatenate([_dot(qqc_sc[pr], sk[i, pair_rows[pr], :].astype(BF16)) for pr in range(n_pairs)],
                                   axis=0) for i in range(pph)]
        sp_pages = [_softplus2(z) for z in z_pages]
        later_pages = [_dot(sp.astype(BF16), tri) for sp in sp_pages]

        alphas, pbs = [], []
        for h in range(A_HEADS):
            m_prev = m_sc[h]
            m_new = jnp.maximum(m_prev, jnp.max(s_heads[h], axis=-1, keepdims=True))
            alpha = jnp.exp2(m_prev - m_new)
            p = jnp.exp2(s_heads[h] - _lane_tile(m_new, pph * page))
            m_sc[h] = m_new
            l_sc[h] = alpha * l_sc[h] + jnp.sum(p, axis=-1, keepdims=True)
            alphas.append(alpha)
            pbs.append(p.astype(BF16))
        carry = jnp.concatenate([carry_sc[pr] for pr in range(n_pairs)], axis=0)
        wbs = []
        for i in range(pph):
            wbs.append(jnp.exp2(z_pages[i] - sp_pages[i] + later_pages[i][:, :page] - carry).astype(BF16))
            carry = carry - later_pages[i][:, page:]
        for pr in range(n_pairs):
            carry_sc[pr] = carry[pr * 2 * t:(pr + 1) * 2 * t]

        for h in range(A_HEADS):
            pv = _dot(pbs[h][:, :page], dv[0, head_rows[h], :].astype(BF16))
            for i in range(1, pph):
                pv = pv + _dot(pbs[h][:, i * page:(i + 1) * page], dv[i, head_rows[h], :].astype(BF16))
            acca_sc[h] = alphas[h] * acca_sc[h] + pv
        for pr in range(n_pairs):
            rows = slice(pr * 2 * t, (pr + 1) * 2 * t)
            pv = _dot_nt(wbs[0][rows], sv[0, pair_rows[pr], :].astype(BF16))
            for i in range(1, pph):
                pv = pv + _dot_nt(wbs[i][rows], sv[i, pair_rows[pr], :].astype(BF16))
            accc_sc[pr] += pv

    half_step(0, b, (2 * j + 1) * pph)
    nxt = jnp.minimum(step + 1, last_step)
    half_step(1, nxt // n_j, (nxt % n_j) * 2 * pph)

    @pl.when(j == n_j - 1)
    def _finish():
        lam = _lambda(lq1_ref, lk1_ref, lq2_ref, lk2_ref, lam_init)
        lane = lax.broadcasted_iota(jnp.int32, (t, LANES), 1)
        for h in range(A_HEADS):
            cols = slice(h * LANES, (h + 1) * LANES)
            o = _diff_finish(acca_sc[h], l_sc[h], lam, sub_ref[...], ga_ref[:, cols].astype(F32), lam_init)
            oa_ref[:, cols] = o.astype(oa_ref.dtype)
        for pr in range(n_pairs):
            cols = slice(pr * LANES, (pr + 1) * LANES)
            acc = accc_sc[pr]
            o = jnp.where(lane < LANES // 2, acc[:t], acc[t:])
            oc_ref[:, cols] = (o * gc_ref[:, cols].astype(F32)).astype(oc_ref.dtype)

    @pl.when(step == last_step)
    def _drain():
        wait_slot(0)


def _decode(layer, page_table, cache_dk, cache_dv, cache_sk, cache_sv,
            q_a, k_a_new, v_a_new, gate_a, q_c, k_c_new, v_c_new, gate_c, subln_g, lams,
            *, dec_batch, dec_seq, pages_per_half, lam_init):
    n_pages = page_table.shape[1]
    depth, n_pool, page = cache_dk.shape[:3]
    pph = pages_per_half
    assert n_pages % (2 * pph) == 0
    n_steps = n_pages // (2 * pph)
    cdk = cache_dk.reshape(depth, n_pool, page * A_HEADS, 2 * A_DHEAD)
    cdv = cache_dv.reshape(depth, n_pool, page * A_HEADS, 2 * A_DHEAD)
    csk = jnp.transpose(cache_sk, (0, 1, 3, 4, 2)).reshape(depth, n_pool, C_WIDTH, page)
    csv = jnp.transpose(cache_sv, (0, 1, 3, 4, 2)).reshape(depth, n_pool, C_WIDTH, page)
    pt_flat = page_table.reshape(-1)

    per_seq = lambda a: a.reshape(dec_batch, dec_seq, a.shape[-1])
    pad_page = lambda a: jnp.pad(per_seq(a), ((0, 0), (0, page - dec_seq), (0, 0)))
    seq_spec = lambda rows, width: pl.BlockSpec((None, rows, width), lambda b, j, pt: (b, 0, 0))
    const_spec = lambda width: pl.BlockSpec((1, width), lambda b, j, pt: (0, 0))
    hbm_spec = pl.BlockSpec(memory_space=pl.ANY)

    in_specs = ([seq_spec(dec_seq, A_WIDTH), seq_spec(page, A_WIDTH), seq_spec(page, A_WIDTH),
                 seq_spec(dec_seq, A_WIDTH),
                 seq_spec(dec_seq, C_WIDTH), seq_spec(page, C_WIDTH), seq_spec(page, C_WIDTH),
                 seq_spec(dec_seq, C_WIDTH),
                 const_spec(LANES)] + [const_spec(A_DHEAD)] * 4 + [hbm_spec] * 4)
    t2 = 2 * dec_seq
    stat_a = pltpu.VMEM((A_HEADS, t2, LANES), F32)
    stat_c = pltpu.VMEM((C_HEADS // 2, t2, LANES), F32)
    page_bufs = [pltpu.VMEM((pph, page * A_HEADS, 2 * A_DHEAD), F32)] * 2 + [pltpu.VMEM((pph, C_WIDTH, page), F32)] * 2
    grid_spec = pltpu.PrefetchScalarGridSpec(
        num_scalar_prefetch=1,
        grid=(dec_batch, n_steps),
        in_specs=in_specs,
        out_specs=[seq_spec(dec_seq, A_WIDTH), seq_spec(dec_seq, C_WIDTH)],
        scratch_shapes=[pltpu.VMEM((A_HEADS, t2, LANES), BF16), pltpu.VMEM((C_HEADS // 2, t2, LANES), BF16),
                        stat_a, stat_a, stat_a, stat_c, stat_c] + page_bufs * 2
                       + [pltpu.SemaphoreType.DMA((2, 4))])
    o_a, o_c = pl.pallas_call(
        functools.partial(_decode_kernel, layer=layer, n_pages=n_pages, pages_per_half=pph, lam_init=lam_init),
        grid_spec=grid_spec,
        out_shape=[jax.ShapeDtypeStruct((dec_batch, dec_seq, A_WIDTH), F32),
                   jax.ShapeDtypeStruct((dec_batch, dec_seq, C_WIDTH), F32)],
        compiler_params=pltpu.CompilerParams(dimension_semantics=("arbitrary", "arbitrary"),
                                             vmem_limit_bytes=VMEM_LIMIT),
        name="decode_attn",
    )(pt_flat, per_seq(q_a), pad_page(k_a_new), pad_page(v_a_new), per_seq(gate_a),
      per_seq(q_c), pad_page(k_c_new), pad_page(v_c_new), per_seq(gate_c),
      subln_g.reshape(1, LANES), *[a.reshape(1, A_DHEAD) for a in lams], cdk, cdv, csk, csv)
    n = dec_batch * dec_seq
    return o_a.reshape(n, A_WIDTH).astype(BF16), o_c.reshape(n, C_WIDTH).astype(BF16)


PROMPT_ROWS = 512
ATTN_BLOCK = 1024
ATTN_ROW_CHUNK = 256
SB_SUB = 256
PAGES_PER_HALF = 16


def kernel(x_prompt, x_sample, cache_diff_k, cache_diff_v, cache_sb_k, cache_sb_v, page_table, w_in, w_out, norm_pre_g, norm_post_g, lambda_q1, lambda_k1, lambda_q2, lambda_k2, diff_subln_g, sgu_ln_g, sgu_ln_b, sgu_w, sgu_b):
    batch, seq, d = x_prompt.shape
    dec_batch, dec_seq, _ = x_sample.shape
    depth = w_in.shape[0]
    w_in_b = w_in.astype(BF16)
    w_out_b = w_out.astype(BF16)
    hp = x_prompt.reshape(batch * seq, d)
    hs = x_sample.reshape(dec_batch * dec_seq, d)
    block = min(ATTN_BLOCK, seq)
    sub = min(SB_SUB, block)
    row_chunk = min(ATTN_ROW_CHUNK, block)
    prompt_rows = min(PROMPT_ROWS, seq)
    new_kv_prompt = None
    outs_s = [[] for _ in range(5)]
    for l in range(depth):
        lam_init = 0.8 - 0.6 * math.exp(-0.3 * l)
        lams = (lambda_q1[l], lambda_k1[l], lambda_q2[l], lambda_k2[l])

        (ak, av, ck, cv, _, qa, ka, va, ga, ob, qc, kc, vc, gc) = _in_proj(
            hp, norm_pre_g[l], w_in_b[l], sgu_ln_g[l], sgu_ln_b[l], sgu_w[l], sgu_b[l],
            seq_len=seq, block_rows=prompt_rows, stacked=(l, depth, new_kv_prompt))
        new_kv_prompt = (ak, av, ck, cv)
        oa = _diff_attn(qa, ka, va, ga, diff_subln_g[l], lams, batch=batch, seq_len=seq, block=block,
                        row_chunk=row_chunk, lam_init=lam_init)
        oc = _sb_attn(qc, kc, vc, gc, batch=batch, seq_len=seq, block=block, sub=sub, row_chunk=row_chunk)
        hp = _out_proj(oa, ob, oc, hp, w_out_b[l], norm_post_g[l], block_rows=prompt_rows)

        (ak, av, ck, cv, vn, qa, ka, va, ga, ob, qc, kc, vc, gc) = _in_proj(
            hs, norm_pre_g[l], w_in_b[l], sgu_ln_g[l], sgu_ln_b[l], sgu_w[l], sgu_b[l],
            seq_len=dec_seq, block_rows=dec_batch * dec_seq)
        oa, oc = _decode(l, page_table, cache_diff_k, cache_diff_v, cache_sb_k, cache_sb_v,
                         qa, ka, va, ga, qc, kc, vc, gc, diff_subln_g[l], lams,
                         dec_batch=dec_batch, dec_seq=dec_seq, pages_per_half=PAGES_PER_HALF,
                         lam_init=lam_init)
        hs = _out_proj(oa, ob, oc, hs, w_out_b[l], norm_post_g[l], block_rows=dec_batch * dec_seq)
        for dst, a, heads in zip(outs_s, (ak, av, ck, cv), (A_HEADS, A_HEADS, C_HEADS, C_HEADS)):
            dst.append(a.reshape(dec_batch, dec_seq, heads, a.shape[-1] // heads))
        outs_s[4].append(vn.reshape(dec_batch, dec_seq, B_WIDTH))

    ak, av, ck, cv = new_kv_prompt
    outs_p = [a.reshape(depth, batch, seq, A_HEADS, 2 * A_DHEAD) for a in (ak, av)]
    outs_p += [jnp.transpose(a.reshape(depth, batch, C_HEADS, C_DHEAD, seq), (0, 1, 4, 2, 3)) for a in (ck, cv)]
    stack = lambda xs: jnp.stack(xs, 0)
    return (hp.reshape(batch, seq, d), hs.reshape(dec_batch, dec_seq, d),
            *outs_p, *[stack(o) for o in outs_s])
```

```python
import functools
import math

import jax
import jax.numpy as jnp
from jax import lax
from jax.experimental import pallas as pl
from jax.experimental.pallas import tpu as pltpu

A_HEADS = 4
A_DHEAD = 64
A_WIDTH = A_HEADS * 2 * A_DHEAD
B_GROUPS = 4
B_DGROUP = 64
B_WIDTH = B_GROUPS * B_DGROUP
CHUNK = 128
C_HEADS = 4
C_DHEAD = 64
C_WIDTH = C_HEADS * C_DHEAD
IN_COLS = 4 * A_WIDTH + 3 * B_WIDTH + 4 * C_WIDTH
EPS = 1e-6
LANES = 128
NEG_BIG = -1e30
LOG2E = 1.4426950408889634
VMEM_LIMIT = 56 * 1024 * 1024

_OFF = {}
_o = 0
for _name, _w in (("a_q", A_WIDTH), ("a_k", A_WIDTH), ("a_v", A_WIDTH), ("a_g", A_WIDTH),
                  ("b_u", B_WIDTH), ("b_v", B_WIDTH), ("b_g", B_WIDTH),
                  ("c_q", C_WIDTH), ("c_k", C_WIDTH), ("c_v", C_WIDTH), ("c_g", C_WIDTH)):
    _OFF[_name] = (_o, _w)
    _o += _w

BF16 = jnp.bfloat16
F32 = jnp.float32


def _silu(x):
    return x / (1.0 + jnp.exp(-x))


def _softplus2(z):
    neg_abs = lax.bitcast_convert_type(lax.bitcast_convert_type(z, jnp.uint32) | jnp.uint32(0x80000000), F32)
    return jnp.maximum(z, 0.0) + jnp.log2(1.0 + jnp.exp2(neg_abs))


def _lane_tile(x, width):
    return x if width == LANES else jnp.concatenate([x] * (width // LANES), axis=1)


def _dot_nt(a, b):
    return lax.dot_general(a, b, (((1,), (1,)), ((), ())), preferred_element_type=F32)


def _dot(a, b):
    return jnp.dot(a, b, preferred_element_type=F32)


def _in_proj_kernel(x_ref, gpre_ref, w_ref, lng_ref, lnb_ref, wt_ref, bias_ref, *rest,
                    chunk_len, mm_rows, cache_layout, n_aliased, own_slab):
    if cache_layout:
        kv_refs = rest[n_aliased:4 + n_aliased]
        if own_slab is not None:
            for ref in kv_refs:
                for slab in range(ref.shape[0]):
                    if slab != own_slab:
                        ref[slab] = jnp.zeros(ref.shape[1:], ref.dtype)
            kv_refs = [ref.at[own_slab] for ref in kv_refs]
        ak_ref, av_ref, ck_ref, cv_ref = kv_refs
        vn_ref = None
        qa_ref, ka_ref, va_ref, ga_ref, ob_ref, qc_ref, kc_ref, vc_ref, gc_ref = rest[4 + n_aliased:]
    else:
        ak_ref, av_ref, ck_ref, cv_ref, vn_ref = rest[:5]
        qa_ref, ka_ref, va_ref, ga_ref, ob_ref, qc_ref, kc_ref, vc_ref, gc_ref = rest[5:]
    tm = x_ref.shape[0]
    x = x_ref[...]
    h = x * lax.rsqrt(jnp.mean(x * x, axis=-1, keepdims=True) + EPS) * gpre_ref[...]
    hb = h.astype(BF16)

    def proj(name):
        off, width = _OFF[name]
        return _dot(hb, w_ref[:, off:off + width])

    def store_a(ref, val):
        if cache_layout:
            for head in range(A_HEADS):
                ref[pl.ds(head, tm, stride=A_HEADS), :] = val[:, head * LANES:(head + 1) * LANES]
        else:
            ref[...] = val

    a_q = proj("a_q")
    qa_ref[...] = (a_q * (A_DHEAD ** -0.5 * LOG2E)).astype(BF16)
    a_k = proj("a_k")
    store_a(ak_ref, a_k)
    ka_ref[...] = a_k.astype(BF16)
    a_v = proj("a_v")
    store_a(av_ref, a_v)
    va_ref[...] = a_v.astype(BF16)
    ga_ref[...] = _silu(proj("a_g")).astype(BF16)
    c_q = proj("c_q")
    qc_ref[...] = (c_q * (C_DHEAD ** -0.5 * LOG2E)).astype(BF16)
    c_k = proj("c_k")
    kc_ref[...] = c_k.astype(BF16)
    c_v = proj("c_v")
    vc_ref[...] = c_v.astype(BF16)
    if cache_layout:
        ck_ref[...] = c_k.T
        cv_ref[...] = c_v.T
    else:
        ck_ref[...] = c_k
        cv_ref[...] = c_v
    gc_ref[...] = _silu(proj("c_g")).astype(BF16)

    b_v = proj("b_v")
    mu = jnp.mean(b_v, axis=-1, keepdims=True)
    cen = b_v - mu
    var = jnp.mean(cen * cen, axis=-1, keepdims=True)
    vn = cen * lax.rsqrt(var + EPS) * lng_ref[...] + lnb_ref[...]
    if vn_ref is not None:
        vn_ref[...] = vn
    vnb = vn.astype(BF16)

    r = lax.broadcasted_iota(jnp.int32, (B_GROUPS * mm_rows, mm_rows), 0) % mm_rows
    c = lax.broadcasted_iota(jnp.int32, (B_GROUPS * mm_rows, mm_rows), 1)
    keep = (r // chunk_len == c // chunk_len) & (c % chunk_len <= r % chunk_len)
    wt = jnp.where(keep, wt_ref[...], 0.0).astype(BF16)
    grp = lax.broadcasted_iota(jnp.int32, (mm_rows, B_WIDTH), 1) // B_DGROUP
    b_u = proj("b_u")
    gate_b = _silu(proj("b_g"))
    for i in range(tm // mm_rows):
        rows = slice(i * mm_rows, (i + 1) * mm_rows)
        full = _dot(wt, vnb[rows])
        mixed = bias_ref[...]
        for g in range(B_GROUPS):
            mixed = mixed + jnp.where(grp == g, full[g * mm_rows:(g + 1) * mm_rows], 0.0)
        ob_ref[rows, :] = (b_u[rows] * mixed * gate_b[rows]).astype(BF16)


def _in_proj(x, g_pre, w_in_bf16, ln_g, ln_b, sgu_w, sgu_b, *, seq_len, block_rows, stacked=None):
    n, d = x.shape
    chunk_len = min(seq_len, CHUNK)
    mm_rows = CHUNK if chunk_len == CHUNK else block_rows
    assert n % block_rows == 0 and block_rows % mm_rows == 0 and mm_rows % chunk_len == 0
    assert seq_len % chunk_len == 0
    rep = mm_rows // chunk_len
    wt = jnp.tile(sgu_w[:, :chunk_len, :chunk_len], (1, rep, rep)).reshape(B_GROUPS * mm_rows, mm_rows)
    bias = jnp.repeat(jnp.tile(sgu_b[:, :chunk_len], (1, rep)).T, B_DGROUP, axis=1)

    row = lambda width: pl.BlockSpec((block_rows, width), lambda i: (i, 0))
    const = lambda shape: pl.BlockSpec(shape, lambda i: (0, 0))
    operands = [x, g_pre.reshape(1, d), w_in_bf16, ln_g.reshape(1, B_WIDTH), ln_b.reshape(1, B_WIDTH), wt, bias]
    in_specs = [row(d), const((1, d)), const((d, IN_COLS)), const((1, B_WIDTH)), const((1, B_WIDTH)),
                const((B_GROUPS * mm_rows, mm_rows)), const((mm_rows, B_WIDTH))]
    bf16_shapes = [jax.ShapeDtypeStruct((n, wd), BF16)
                   for wd in (A_WIDTH, A_WIDTH, A_WIDTH, A_WIDTH, B_WIDTH, C_WIDTH, C_WIDTH, C_WIDTH, C_WIDTH)]
    aliases = {}
    own_slab = None
    if stacked is None:
        f32_shapes = [jax.ShapeDtypeStruct((n, wd), F32) for wd in (A_WIDTH, A_WIDTH, C_WIDTH, C_WIDTH, B_WIDTH)]
        f32_specs = [row(s.shape[1]) for s in f32_shapes]
        n_aliased = 0
    else:
        layer, depth, previous = stacked
        assert seq_len % block_rows == 0
        batch, per_seq = n // seq_len, seq_len // block_rows
        f32_shapes = ([jax.ShapeDtypeStruct((depth, n * A_HEADS, 2 * A_DHEAD), F32)] * 2
                      + [jax.ShapeDtypeStruct((depth, batch, C_WIDTH, seq_len), F32)] * 2)
        if previous is None:
            own_slab = layer
            f32_specs = ([pl.BlockSpec((depth, block_rows * A_HEADS, 2 * A_DHEAD), lambda i: (0, i, 0))] * 2
                         + [pl.BlockSpec((depth, None, C_WIDTH, block_rows),
                                         lambda i: (0, i // per_seq, 0, i % per_seq))] * 2)
        else:
            f32_specs = ([pl.BlockSpec((None, block_rows * A_HEADS, 2 * A_DHEAD), lambda i: (layer, i, 0))] * 2
                         + [pl.BlockSpec((None, None, C_WIDTH, block_rows),
                                         lambda i: (layer, i // per_seq, 0, i % per_seq))] * 2)
        n_aliased = 0 if previous is None else len(previous)
        for i in range(n_aliased):
            aliases[len(operands)] = i
            operands.append(previous[i])
            in_specs.append(pl.BlockSpec(memory_space=pl.ANY))
    out_shapes = f32_shapes + bf16_shapes
    outs = pl.pallas_call(
        functools.partial(_in_proj_kernel, chunk_len=chunk_len, mm_rows=mm_rows,
                          cache_layout=stacked is not None, n_aliased=n_aliased, own_slab=own_slab),
        grid=(n // block_rows,),
        in_specs=in_specs,
        out_specs=f32_specs + [row(s.shape[1]) for s in bf16_shapes],
        out_shape=out_shapes,
        input_output_aliases=aliases,
        compiler_params=pltpu.CompilerParams(dimension_semantics=("parallel",),
                                             vmem_limit_bytes=VMEM_LIMIT),
        name="in_proj",
    )(*operands)
    if stacked is not None:
        outs = list(outs[:4]) + [None] + list(outs[4:])
    return tuple(outs)


def _out_proj_kernel(oa_ref, ob_ref, oc_ref, x_ref, w_ref, g_ref, y_ref):
    mix = (_dot(oa_ref[...], w_ref[0:A_WIDTH, :])
           + _dot(ob_ref[...], w_ref[A_WIDTH:A_WIDTH + B_WIDTH, :])
           + _dot(oc_ref[...], w_ref[A_WIDTH + B_WIDTH:, :]))
    y_ref[...] = x_ref[...] + mix * lax.rsqrt(jnp.mean(mix * mix, axis=-1, keepdims=True) + EPS) * g_ref[...]


def _out_proj(o_a, o_b, o_c, x, w_out_bf16, g_post, *, block_rows):
    n, d = x.shape
    row = lambda width: pl.BlockSpec((block_rows, width), lambda i: (i, 0))
    return pl.pallas_call(
        _out_proj_kernel,
        grid=(n // block_rows,),
        in_specs=[row(A_WIDTH), row(B_WIDTH), row(C_WIDTH), row(d),
                  pl.BlockSpec(w_out_bf16.shape, lambda i: (0, 0)),
                  pl.BlockSpec((1, d), lambda i: (0, 0))],
        out_specs=row(d),
        out_shape=jax.ShapeDtypeStruct((n, d), F32),
        compiler_params=pltpu.CompilerParams(dimension_semantics=("parallel",),
                                             vmem_limit_bytes=VMEM_LIMIT),
        name="out_proj",
    )(o_a, o_b, o_c, x, w_out_bf16, g_post.reshape(1, d))


def _split_halves(q):
    lane = lax.broadcasted_iota(jnp.int32, q.shape, 1)
    zero = jnp.zeros_like(q)
    return jnp.concatenate([jnp.where(lane < LANES // 2, q, zero), jnp.where(lane >= LANES // 2, q, zero)], axis=0)


def _lambda(lq1_ref, lk1_ref, lq2_ref, lk2_ref, lam_init):
    s1 = jnp.sum(lq1_ref[...] * lk1_ref[...], axis=-1, keepdims=True)
    s2 = jnp.sum(lq2_ref[...] * lk2_ref[...], axis=-1, keepdims=True)
    return jnp.exp(s1) - jnp.exp(s2) + lam_init


def _diff_finish(acc, l, lam, subln_g, gate, lam_init):
    r = acc.shape[0] // 2
    o = acc[:r] / l[:r] - lam * (acc[r:] / l[r:])
    o = o * lax.rsqrt(jnp.mean(o * o, axis=-1, keepdims=True) + EPS) * subln_g * (1.0 - lam_init)
    return o * gate


def _softmax_update(s, m_prev, l_prev, acc_prev, v_blocks):
    m_new = jnp.maximum(m_prev, jnp.max(s, axis=-1, keepdims=True))
    alpha = jnp.exp2(m_prev - m_new)
    p = jnp.exp2(s - _lane_tile(m_new, s.shape[1]))
    pb = p.astype(BF16)
    pv, start = None, 0
    for v in v_blocks:
        part = _dot(pb[:, start:start + v.shape[0]], v)
        pv = part if pv is None else pv + part
        start += v.shape[0]
    if pv.shape[1] == 2 * LANES:
        row_sum, pv = pv[:, LANES:], pv[:, :LANES]
    else:
        row_sum = jnp.sum(p, axis=-1, keepdims=True)
    return m_new, alpha * l_prev + row_sum, alpha * acc_prev + pv


def _with_ones(v):
    return jnp.concatenate([v, jnp.ones(v.shape, v.dtype)], axis=1)


def _triangle_tables(n_blocks, descending):
    qs, ks = [], []
    for qi in range(n_blocks):
        order = range(qi, -1, -1) if descending else range(qi + 1)
        for ki in order:
            qs.append(qi)
            ks.append(ki)
    return jnp.asarray(qs, jnp.int32), jnp.asarray(ks, jnp.int32)


def _diff_attn_kernel(qtab_ref, ktab_ref, q_ref, k_ref, v_ref, gate_ref, sub_ref,
                      lq1_ref, lk1_ref, lq2_ref, lk2_ref, o_ref,
                      qq_sc, m_sc, l_sc, acc_sc, *, lam_init, row_chunk):
    s_idx = pl.program_id(2)
    qi = qtab_ref[s_idx]
    ki = ktab_ref[s_idx]
    tq = q_ref.shape[0]
    tk = k_ref.shape[0]

    @pl.when(ki == 0)
    def _init():
        qq_sc[...] = _split_halves(q_ref[...])
        m_sc[...] = jnp.full(m_sc.shape, NEG_BIG, F32)
        l_sc[...] = jnp.zeros(l_sc.shape, F32)
        acc_sc[...] = jnp.zeros(acc_sc.shape, F32)

    def step(masked):
        k = k_ref[...]
        v_ones = _with_ones(v_ref[...])
        chunks = []
        for c in range(2 * tq // row_chunk):
            rows = slice(c * row_chunk, (c + 1) * row_chunk)
            q0 = (c * row_chunk) % tq
            width = min(tk, q0 + row_chunk) if masked else tk
            s = _dot_nt(qq_sc[rows], k[:width])
            if masked:
                qpos = q0 + lax.broadcasted_iota(jnp.int32, s.shape, 0)
                kpos = lax.broadcasted_iota(jnp.int32, s.shape, 1)
                s = jnp.where(kpos <= qpos, s, NEG_BIG)
            chunks.append((rows, width, s))
        weights = []
        for rows, width, s in chunks:
            m_prev = m_sc[rows]
            m_new = jnp.maximum(m_prev, jnp.max(s, axis=-1, keepdims=True))
            m_sc[rows] = m_new
            weights.append((jnp.exp2(m_prev - m_new), jnp.exp2(s - _lane_tile(m_new, width)).astype(BF16)))
        for (rows, width, _), (alpha, pb) in zip(chunks, weights):
            pv = _dot(pb, v_ones[:width])
            l_sc[rows] = alpha * l_sc[rows] + pv[:, LANES:]
            acc_sc[rows] = alpha * acc_sc[rows] + pv[:, :LANES]

    @pl.when(ki < qi)
    def _off_diag():
        step(False)

    @pl.when(ki == qi)
    def _diag():
        step(True)
        lam = _lambda(lq1_ref, lk1_ref, lq2_ref, lk2_ref, lam_init)
        o = _diff_finish(acc_sc[...], l_sc[...], lam, sub_ref[...], gate_ref[...].astype(F32), lam_init)
        o_ref[...] = o.astype(o_ref.dtype)


def _diff_attn(q, k, v, gate, subln_g, lams, *, batch, seq_len, block, row_chunk, lam_init):
    n = q.shape[0]
    nb = seq_len // block
    qtab, ktab = _triangle_tables(nb, descending=False)
    qmap = lambda b, h, s, qt, kt: (b * nb + qt[s], h)
    kmap = lambda b, h, s, qt, kt: (b * nb + kt[s], h)
    cmap = lambda b, h, s, qt, kt: (0, 0)
    blk = lambda m: pl.BlockSpec((block, LANES), m)
    lam_spec = pl.BlockSpec((1, A_DHEAD), cmap)
    stat = pltpu.VMEM((2 * block, LANES), F32)
    grid_spec = pltpu.PrefetchScalarGridSpec(
        num_scalar_prefetch=2,
        grid=(batch, A_HEADS, int(qtab.shape[0])),
        in_specs=[blk(qmap), blk(kmap), blk(kmap), blk(qmap), pl.BlockSpec((1, LANES), cmap),
                  lam_spec, lam_spec, lam_spec, lam_spec],
        out_specs=blk(qmap),
        scratch_shapes=[pltpu.VMEM((2 * block, LANES), BF16), stat, stat, stat])
    return pl.pallas_call(
        functools.partial(_diff_attn_kernel, lam_init=lam_init, row_chunk=row_chunk),
        grid_spec=grid_spec,
        out_shape=jax.ShapeDtypeStruct((n, A_WIDTH), BF16),
        compiler_params=pltpu.CompilerParams(dimension_semantics=("parallel", "parallel", "arbitrary"),
                                             vmem_limit_bytes=VMEM_LIMIT),
        name="diff_attn",
    )(qtab, ktab, q, k, v, gate, subln_g.reshape(1, LANES), *[a.reshape(1, A_DHEAD) for a in lams])


def _neg_later(n, with_total):
    j = lax.broadcasted_iota(jnp.int32, (n, n), 0)
    k = lax.broadcasted_iota(jnp.int32, (n, n), 1)
    m = jnp.where(j > k, -1.0, 0.0).astype(BF16)
    return jnp.concatenate([m, jnp.full((n, LANES), -1.0, BF16)], axis=1) if with_total else m


def _sb_attn_kernel(qtab_ref, ktab_ref, q_ref, k_ref, v_ref, gate_ref, o_ref,
                    qq_sc, carry_sc, acc_sc, *, sub, row_chunk):
    s_idx = pl.program_id(2)
    qi = qtab_ref[s_idx]
    ki = ktab_ref[s_idx]
    tq = q_ref.shape[0]
    tk = k_ref.shape[0]

    @pl.when(ki == qi)
    def _init():
        qq_sc[...] = _split_halves(q_ref[...])
        carry_sc[...] = jnp.zeros(carry_sc.shape, F32)
        acc_sc[...] = jnp.zeros(acc_sc.shape, F32)

    def step(masked):
        k = k_ref[...]
        v = v_ref[...]
        tri = _neg_later(sub, with_total=False)
        chunks = []
        for c in range(2 * tq // row_chunk):
            rows = slice(c * row_chunk, (c + 1) * row_chunk)
            q0 = (c * row_chunk) % tq
            width = min(tk, -(-(q0 + row_chunk) // sub) * sub) if masked else tk
            z = _dot_nt(qq_sc[rows], k[:width])
            valid = None
            if masked:
                qpos = q0 + lax.broadcasted_iota(jnp.int32, z.shape, 0)
                kpos = lax.broadcasted_iota(jnp.int32, z.shape, 1)
                valid = kpos < qpos
            chunks.append((rows, width, z, valid))
        prefix = []
        for rows, width, z, valid in chunks:
            sp = _softplus2(z)
            if valid is not None:
                sp = jnp.where(valid, sp, 0.0)
            later = [_dot(sp[:, j * sub:(j + 1) * sub].astype(BF16), tri) for j in range(width // sub)]
            prefix.append((sp, later))
        weights = []
        for (rows, width, z, valid), (sp, later) in zip(chunks, prefix):
            carry = carry_sc[rows]
            ws = [None] * (width // sub)
            for j in range(width // sub - 1, -1, -1):
                cols = slice(j * sub, (j + 1) * sub)
                t = z[:, cols] - sp[:, cols] + later[j] - _lane_tile(carry, sub)
                if valid is not None:
                    t = jnp.where(valid[:, cols], t, NEG_BIG)
                ws[j] = jnp.exp2(t)
                carry = carry + jnp.sum(sp[:, cols], axis=-1, keepdims=True)
            carry_sc[rows] = carry
            weights.append(jnp.concatenate(ws, axis=1).astype(BF16))
        for (rows, width, _, _), wb in zip(chunks, weights):
            acc_sc[rows] += _dot(wb, v[:width])

    @pl.when(ki == qi)
    def _diag():
        step(True)

    @pl.when(ki < qi)
    def _off_diag():
        step(False)

    @pl.when(ki == 0)
    def _finish():
        acc = acc_sc[...]
        lane = lax.broadcasted_iota(jnp.int32, (tq, LANES), 1)
        o = jnp.where(lane < LANES // 2, acc[:tq], acc[tq:])
        o_ref[...] = (o * gate_ref[...].astype(F32)).astype(o_ref.dtype)


def _sb_attn(q, k, v, gate, *, batch, seq_len, block, sub, row_chunk):
    n = q.shape[0]
    nb = seq_len // block
    qtab, ktab = _triangle_tables(nb, descending=True)
    qmap = lambda b, h, s, qt, kt: (b * nb + qt[s], h)
    kmap = lambda b, h, s, qt, kt: (b * nb + kt[s], h)
    blk = lambda m: pl.BlockSpec((block, LANES), m)
    grid_spec = pltpu.PrefetchScalarGridSpec(
        num_scalar_prefetch=2,
        grid=(batch, C_HEADS // 2, int(qtab.shape[0])),
        in_specs=[blk(qmap), blk(kmap), blk(kmap), blk(qmap)],
        out_specs=blk(qmap),
        scratch_shapes=[pltpu.VMEM((2 * block, LANES), BF16), pltpu.VMEM((2 * block, LANES), F32),
                        pltpu.VMEM((2 * block, LANES), F32)])
    return pl.pallas_call(
        functools.partial(_sb_attn_kernel, sub=sub, row_chunk=row_chunk),
        grid_spec=grid_spec,
        out_shape=jax.ShapeDtypeStruct((n, C_WIDTH), BF16),
        compiler_params=pltpu.CompilerParams(dimension_semantics=("parallel", "parallel", "arbitrary"),
                                             vmem_limit_bytes=VMEM_LIMIT),
        name="sb_attn",
    )(qtab, ktab, q, k, v, gate)


def _decode_kernel(pt_ref, qa_ref, kan_ref, van_ref, ga_ref, qc_ref, kcn_ref, vcn_ref, gc_ref, sub_ref,
                   lq1_ref, lk1_ref, lq2_ref, lk2_ref, cdk_ref, cdv_ref, csk_ref, csv_ref, oa_ref, oc_ref,
                   qqa_sc, qqc_sc, m_sc, l_sc, acca_sc, carry_sc, accc_sc,
                   dk_a, dv_a, sk_a, sv_a, dk_b, dv_b, sk_b, sv_b, sems,
                   *, layer, n_pages, pages_per_half, lam_init):
    pph = pages_per_half
    b = pl.program_id(0)
    j = pl.program_id(1)
    n_j = pl.num_programs(1)
    step = b * n_j + j
    last_step = pl.num_programs(0) * n_j - 1
    t = qa_ref.shape[0]
    page = sk_a.shape[2]
    n_pairs = C_HEADS // 2
    tri = _neg_later(page, with_total=True)
    caches = (cdk_ref, cdv_ref, csk_ref, csv_ref)
    buffers = ((dk_a, dv_a, sk_a, sv_a), (dk_b, dv_b, sk_b, sv_b))

    def page_copy(slot, kind, i, page_id):
        return pltpu.make_async_copy(caches[kind].at[layer, page_id], buffers[slot][kind].at[i], sems.at[slot, kind])

    def start_page(slot, seq, first_page, i):
        page_id = pt_ref[seq * n_pages + (n_pages - 1 - (first_page + i))]
        for kind in range(4):
            page_copy(slot, kind, i, page_id).start(priority=kind % 2)

    def wait_slot(slot):
        for i in range(pph):
            for kind in range(4):
                page_copy(slot, kind, i, 0).wait()

    def sb_page(z, valid, carry):
        sp = _softplus2(z)
        if valid is not None:
            sp = jnp.where(valid, sp, 0.0)
        later = _dot(sp.astype(BF16), tri)
        tt = z - sp + later[:, :page] - carry
        if valid is not None:
            tt = jnp.where(valid, tt, NEG_BIG)
        return jnp.exp2(tt), carry - later[:, page:]

    @pl.when(step == 0)
    def _prologue():
        for i in range(pph):
            start_page(0, 0, 0, i)

    @pl.when(j == 0)
    def _first():
        qpos = lax.broadcasted_iota(jnp.int32, (2 * t, page), 0) % t
        kpos = lax.broadcasted_iota(jnp.int32, (2 * t, page), 1)
        kan = kan_ref[...]
        van = van_ref[...]
        for h in range(A_HEADS):
            cols = slice(h * LANES, (h + 1) * LANES)
            qq = _split_halves(qa_ref[:, cols])
            qqa_sc[h] = qq
            s = jnp.where(kpos <= qpos, _dot_nt(qq, kan[:, cols]), NEG_BIG)
            m_sc[h], l_sc[h], acca_sc[h] = _softmax_update(
                s, jnp.full((2 * t, LANES), NEG_BIG, F32), jnp.zeros((2 * t, LANES), F32),
                jnp.zeros((2 * t, LANES), F32), [_with_ones(van[:, cols])])
        kcn = kcn_ref[...]
        vcn = vcn_ref[...]
        for pr in range(n_pairs):
            cols = slice(pr * LANES, (pr + 1) * LANES)
            qq = _split_halves(qc_ref[:, cols])
            qqc_sc[pr] = qq
            w, carry_sc[pr] = sb_page(_dot_nt(qq, kcn[:, cols]), kpos < qpos, jnp.zeros((2 * t, LANES), F32))
            accc_sc[pr] = _dot(w.astype(BF16), vcn[:, cols])

    head_rows = [pl.ds(h, page, stride=A_HEADS) for h in range(A_HEADS)]
    pair_rows = [slice(pr * LANES, (pr + 1) * LANES) for pr in range(n_pairs)]

    def half_step(slot, next_seq, next_first_page):
        dk, dv, sk, sv = buffers[slot]
        for i in range(pph):
            start_page(1 - slot, next_seq, next_first_page, i)
        wait_slot(slot)
        s_heads = [jnp.concatenate([_dot_nt(qqa_sc[h], dk[i, head_rows[h], :].astype(BF16)) for i in range(pph)], axis=1)
                   for h in range(A_HEADS)]
        z_pages = [jnp.concatenate([_dot(qqc_sc[pr], sk[i, pair_rows[pr], :].astype(BF16)) for pr in range(n_pairs)],
                                   axis=0) for i in range(pph)]
        sp_pages = [_softplus2(z) for z in z_pages]
        later_pages = [_dot(sp.astype(BF16), tri) for sp in sp_pages]

        alphas, pbs = [], []
        for h in range(A_HEADS):
            m_prev = m_sc[h]
            m_new = jnp.maximum(m_prev, jnp.max(s_heads[h], axis=-1, keepdims=True))
            alpha = jnp.exp2(m_prev - m_new)
            p = jnp.exp2(s_heads[h] - _lane_tile(m_new, pph * page))
            m_sc[h] = m_new
            l_sc[h] = alpha * l_sc[h] + jnp.sum(p, axis=-1, keepdims=True)
            alphas.append(alpha)
            pbs.append(p.astype(BF16))
        carry = jnp.concatenate([carry_sc[pr] for pr in range(n_pairs)], axis=0)
        wbs = []
        for i in range(pph):
            wbs.append(jnp.exp2(z_pages[i] - sp_pages[i] + later_pages[i][:, :page] - carry).astype(BF16))
            carry = carry - later_pages[i][:, page:]
        for pr in range(n_pairs):
            carry_sc[pr] = carry[pr * 2 * t:(pr + 1) * 2 * t]

        for h in range(A_HEADS):
            pv = _dot(pbs[h][:, :page], dv[0, head_rows[h], :].astype(BF16))
            for i in range(1, pph):
                pv = pv + _dot(pbs[h][:, i * page:(i + 1) * page], dv[i, head_rows[h], :].astype(BF16))
            acca_sc[h] = alphas[h] * acca_sc[h] + pv
        for pr in range(n_pairs):
            rows = slice(pr * 2 * t, (pr + 1) * 2 * t)
            pv = _dot_nt(wbs[0][rows], sv[0, pair_rows[pr], :].astype(BF16))
            for i in range(1, pph):
                pv = pv + _dot_nt(wbs[i][rows], sv[i, pair_rows[pr], :].astype(BF16))
            accc_sc[pr] += pv

    half_step(0, b, (2 * j + 1) * pph)
    nxt = jnp.minimum(step + 1, last_step)
    half_step(1, nxt // n_j, (nxt % n_j) * 2 * pph)

    @pl.when(j == n_j - 1)
    def _finish():
        lam = _lambda(lq1_ref, lk1_ref, lq2_ref, lk2_ref, lam_init)
        lane = lax.broadcasted_iota(jnp.int32, (t, LANES), 1)
        for h in range(A_HEADS):
            cols = slice(h * LANES, (h + 1) * LANES)
            o = _diff_finish(acca_sc[h], l_sc[h], lam, sub_ref[...], ga_ref[:, cols].astype(F32), lam_init)
            oa_ref[:, cols] = o.astype(oa_ref.dtype)
        for pr in range(n_pairs):
            cols = slice(pr * LANES, (pr + 1) * LANES)
            acc = accc_sc[pr]
            o = jnp.where(lane < LANES // 2, acc[:t], acc[t:])
            oc_ref[:, cols] = (o * gc_ref[:, cols].astype(F32)).astype(oc_ref.dtype)

    @pl.when(step == last_step)
    def _drain():
        wait_slot(0)


def _decode(layer, page_table, cache_dk, cache_dv, cache_sk, cache_sv,
            q_a, k_a_new, v_a_new, gate_a, q_c, k_c_new, v_c_new, gate_c, subln_g, lams,
            *, dec_batch, dec_seq, pages_per_half, lam_init):
    n_pages = page_table.shape[1]
    depth, n_pool, page = cache_dk.shape[:3]
    pph = pages_per_half
    assert n_pages % (2 * pph) == 0
    n_steps = n_pages // (2 * pph)
    cdk = cache_dk.reshape(depth, n_pool, page * A_HEADS, 2 * A_DHEAD)
    cdv = cache_dv.reshape(depth, n_pool, page * A_HEADS, 2 * A_DHEAD)
    csk = jnp.transpose(cache_sk, (0, 1, 3, 4, 2)).reshape(depth, n_pool, C_WIDTH, page)
    csv = jnp.transpose(cache_sv, (0, 1, 3, 4, 2)).reshape(depth, n_pool, C_WIDTH, page)
    pt_flat = page_table.reshape(-1)

    per_seq = lambda a: a.reshape(dec_batch, dec_seq, a.shape[-1])
    pad_page = lambda a: jnp.pad(per_seq(a), ((0, 0), (0, page - dec_seq), (0, 0)))
    seq_spec = lambda rows, width: pl.BlockSpec((None, rows, width), lambda b, j, pt: (b, 0, 0))
    const_spec = lambda width: pl.BlockSpec((1, width), lambda b, j, pt: (0, 0))
    hbm_spec = pl.BlockSpec(memory_space=pl.ANY)

    in_specs = ([seq_spec(dec_seq, A_WIDTH), seq_spec(page, A_WIDTH), seq_spec(page, A_WIDTH),
                 seq_spec(dec_seq, A_WIDTH),
                 seq_spec(dec_seq, C_WIDTH), seq_spec(page, C_WIDTH), seq_spec(page, C_WIDTH),
                 seq_spec(dec_seq, C_WIDTH),
                 const_spec(LANES)] + [const_spec(A_DHEAD)] * 4 + [hbm_spec] * 4)
    t2 = 2 * dec_seq
    stat_a = pltpu.VMEM((A_HEADS, t2, LANES), F32)
    stat_c = pltpu.VMEM((C_HEADS // 2, t2, LANES), F32)
    page_bufs = [pltpu.VMEM((pph, page * A_HEADS, 2 * A_DHEAD), F32)] * 2 + [pltpu.VMEM((pph, C_WIDTH, page), F32)] * 2
    grid_spec = pltpu.PrefetchScalarGridSpec(
        num_scalar_prefetch=1,
        grid=(dec_batch, n_steps),
        in_specs=in_specs,
        out_specs=[seq_spec(dec_seq, A_WIDTH), seq_spec(dec_seq, C_WIDTH)],
        scratch_shapes=[pltpu.VMEM((A_HEADS, t2, LANES), BF16), pltpu.VMEM((C_HEADS // 2, t2, LANES), BF16),
                        stat_a, stat_a, stat_a, stat_c, stat_c] + page_bufs * 2
                       + [pltpu.SemaphoreType.DMA((2, 4))])
    o_a, o_c = pl.pallas_call(
        functools.partial(_decode_kernel, layer=layer, n_pages=n_pages, pages_per_half=pph, lam_init=lam_init),
        grid_spec=grid_spec,
        out_shape=[jax.ShapeDtypeStruct((dec_batch, dec_seq, A_WIDTH), F32),
                   jax.ShapeDtypeStruct((dec_batch, dec_seq, C_WIDTH), F32)],
        compiler_params=pltpu.CompilerParams(dimension_semantics=("arbitrary", "arbitrary"),
                                             vmem_limit_bytes=VMEM_LIMIT),
        name="decode_attn",
    )(pt_flat, per_seq(q_a), pad_page(k_a_new), pad_page(v_a_new), per_seq(gate_a),
      per_seq(q_c), pad_page(k_c_new), pad_page(v_c_new), per_seq(gate_c),
      subln_g.reshape(1, LANES), *[a.reshape(1, A_DHEAD) for a in lams], cdk, cdv, csk, csv)
    n = dec_batch * dec_seq
    return o_a.reshape(n, A_WIDTH).astype(BF16), o_c.reshape(n, C_WIDTH).astype(BF16)


PROMPT_ROWS = 512
ATTN_BLOCK = 1024
ATTN_ROW_CHUNK = 256
SB_SUB = 256
PAGES_PER_HALF = 16


def kernel(x_prompt, x_sample, cache_diff_k, cache_diff_v, cache_sb_k, cache_sb_v, page_table, w_in, w_out, norm_pre_g, norm_post_g, lambda_q1, lambda_k1, lambda_q2, lambda_k2, diff_subln_g, sgu_ln_g, sgu_ln_b, sgu_w, sgu_b):
    batch, seq, d = x_prompt.shape
    dec_batch, dec_seq, _ = x_sample.shape
    depth = w_in.shape[0]
    w_in_b = w_in.astype(BF16)
    w_out_b = w_out.astype(BF16)
    hp = x_prompt.reshape(batch * seq, d)
    hs = x_sample.reshape(dec_batch * dec_seq, d)
    block = min(ATTN_BLOCK, seq)
    sub = min(SB_SUB, block)
    row_chunk = min(ATTN_ROW_CHUNK, block)
    prompt_rows = min(PROMPT_ROWS, seq)
    new_kv_prompt = None
    outs_s = [[] for _ in range(5)]
    for l in range(depth):
        lam_init = 0.8 - 0.6 * math.exp(-0.3 * l)
        lams = (lambda_q1[l], lambda_k1[l], lambda_q2[l], lambda_k2[l])

        (ak, av, ck, cv, _, qa, ka, va, ga, ob, qc, kc, vc, gc) = _in_proj(
            hp, norm_pre_g[l], w_in_b[l], sgu_ln_g[l], sgu_ln_b[l], sgu_w[l], sgu_b[l],
            seq_len=seq, block_rows=prompt_rows, stacked=(l, depth, new_kv_prompt))
        new_kv_prompt = (ak, av, ck, cv)
        oa = _diff_attn(qa, ka, va, ga, diff_subln_g[l], lams, batch=batch, seq_len=seq, block=block,
                        row_chunk=row_chunk, lam_init=lam_init)
        oc = _sb_attn(qc, kc, vc, gc, batch=batch, seq_len=seq, block=block, sub=sub, row_chunk=row_chunk)
        hp = _out_proj(oa, ob, oc, hp, w_out_b[l], norm_post_g[l], block_rows=prompt_rows)

        (ak, av, ck, cv, vn, qa, ka, va, ga, ob, qc, kc, vc, gc) = _in_proj(
            hs, norm_pre_g[l], w_in_b[l], sgu_ln_g[l], sgu_ln_b[l], sgu_w[l], sgu_b[l],
            seq_len=dec_seq, block_rows=dec_batch * dec_seq)
        oa, oc = _decode(l, page_table, cache_diff_k, cache_diff_v, cache_sb_k, cache_sb_v,
                         qa, ka, va, ga, qc, kc, vc, gc, diff_subln_g[l], lams,
                         dec_batch=dec_batch, dec_seq=dec_seq, pages_per_half=PAGES_PER_HALF,
                         lam_init=lam_init)
        hs = _out_proj(oa, ob, oc, hs, w_out_b[l], norm_post_g[l], block_rows=dec_batch * dec_seq)
        for dst, a, heads in zip(outs_s, (ak, av, ck, cv), (A_HEADS, A_HEADS, C_HEADS, C_HEADS)):
            dst.append(a.reshape(dec_batch, dec_seq, heads, a.shape[-1] // heads))
        outs_s[4].append(vn.reshape(dec_batch, dec_seq, B_WIDTH))

    ak, av, ck, cv = new_kv_prompt
    outs_p = [a.reshape(depth, batch, seq, A_HEADS, 2 * A_DHEAD) for a in (ak, av)]
    outs_p += [jnp.transpose(a.reshape(depth, batch, C_HEADS, C_DHEAD, seq), (0, 1, 4, 2, 3)) for a in (ck, cv)]
    stack = lambda xs: jnp.stack(xs, 0)
    return (hp.reshape(batch, seq, d), hs.reshape(dec_batch, dec_seq, d),
            *outs_p, *[stack(o) for o in outs_s])
```
